```python
import math
import jax, jax.numpy as jnp
from jax import lax
import numpy as np

D_MODEL = 2048
BATCH = 4
SEQ = 2048
DEPTH = 2

N_MIXERS = 2
CHUNK = 128
A_WIDTH = D_MODEL
A_GROUPS = 8
A_HEAD = A_WIDTH // A_GROUPS
B_WIDTH = D_MODEL
B_WINDOWS = (2, 4, 8, 16)
B_GROUPS = len(B_WINDOWS)
B_HEAD = B_WIDTH // B_GROUPS
D_FF = 4 * D_MODEL
N_A = (DEPTH + 1) // 2
N_B = DEPTH // 2
EPS = 1e-6

kernel_name = "hybrid_chunked_gmlp_multiscale_pool"


def rmsnorm(x, g):
    x32 = x.astype(jnp.float32)
    y = x32 * lax.rsqrt(jnp.mean(x32 * x32, axis=-1, keepdims=True) + EPS)
    return (y * g.astype(jnp.float32)).astype(x.dtype)


def layernorm(x, g, b):
    x32 = x.astype(jnp.float32)
    mu = jnp.mean(x32, axis=-1, keepdims=True)
    xc = x32 - mu
    y = xc * lax.rsqrt(jnp.mean(xc * xc, axis=-1, keepdims=True) + EPS)
    return (y * g.astype(jnp.float32) + b.astype(jnp.float32)).astype(x.dtype)


def chunked_gmlp_mixer(h, w_in, ln_g, ln_b, w_s, b_s, w_out):
    bsz, seq, _ = h.shape
    n_chunks = seq // CHUNK
    z = jax.nn.gelu(h @ w_in, approximate=False)
    u, v = jnp.split(z, 2, axis=-1)
    v = layernorm(v, ln_g, ln_b)
    v = v.reshape(bsz, n_chunks, CHUNK, A_GROUPS, A_HEAD)
    causal = jnp.tril(jnp.ones((CHUNK, CHUNK), dtype=w_s.dtype))
    w_masked = w_s * causal[None]
    s = jnp.einsum('gts,bcsgd->bctgd', w_masked, v) + jnp.transpose(b_s)[None, None, :, :, None]
    gated = u * s.reshape(bsz, seq, A_WIDTH)
    return gated @ w_out


def causal_window_mean(v, w):
    seq = v.shape[1]
    v32 = v.astype(jnp.float32)
    c = jnp.cumsum(v32, axis=1)
    c_prev = jnp.pad(c, ((0, 0), (w, 0), (0, 0)))[:, :seq]
    count = jnp.minimum(jnp.arange(1, seq + 1), w).astype(jnp.float32)
    return ((c - c_prev) / count[None, :, None]).astype(v.dtype)


def multiscale_pool_mixer(h, w_in, w_grp, scale, w_out):
    bsz, seq, _ = h.shape
    v = (h @ w_in).reshape(bsz, seq, B_GROUPS, B_HEAD)
    pooled = jnp.stack(
        [causal_window_mean(v[:, :, g], w) - v[:, :, g] for g, w in enumerate(B_WINDOWS)],
        axis=2)
    mixed = jnp.einsum('bsgc,gcd->bsgd', pooled, w_grp).reshape(bsz, seq, B_WIDTH)
    return (mixed * scale) @ w_out


def sqrelu_mlp(h, w1, w2):
    a = jax.nn.relu(h @ w1)
    return (a * a) @ w2


def setup_inputs(seed: int = 0) -> dict:
    key = jax.random.key(seed)
    ks = jax.random.split(key, 20)
    f32 = jnp.float32

    def nrm(k, shape, scale):
        return jax.random.normal(k, shape, f32) * scale

    x = jax.random.normal(ks[0], (BATCH, SEQ, D_MODEL), f32)
    a_w_in = nrm(ks[1], (N_A, D_MODEL, 2 * A_WIDTH), D_MODEL ** -0.5)
    a_ln_g = 1.0 + nrm(ks[2], (N_A, A_WIDTH), 0.02)
    a_ln_b = nrm(ks[3], (N_A, A_WIDTH), 0.02)
    a_w_s = nrm(ks[4], (N_A, A_GROUPS, CHUNK, CHUNK), 0.5 * CHUNK ** -0.5)
    a_b_s = 1.0 + nrm(ks[5], (N_A, A_GROUPS, CHUNK), 0.02)
    a_w_out = nrm(ks[6], (N_A, A_WIDTH, D_MODEL), A_WIDTH ** -0.5)
    b_w_in = nrm(ks[7], (N_B, D_MODEL, B_WIDTH), D_MODEL ** -0.5)
    b_w_grp = nrm(ks[8], (N_B, B_GROUPS, B_HEAD, B_HEAD), B_HEAD ** -0.5)
    b_scale = 0.5 + nrm(ks[9], (N_B, B_WIDTH), 0.02)
    b_w_out = nrm(ks[10], (N_B, B_WIDTH, D_MODEL), B_WIDTH ** -0.5)
    norm_mix = 1.0 + nrm(ks[11], (DEPTH, D_MODEL), 0.02)
    norm_mlp = 1.0 + nrm(ks[12], (DEPTH, D_MODEL), 0.02)
    mlp_w1 = nrm(ks[13], (DEPTH, D_MODEL, D_FF), D_MODEL ** -0.5)
    mlp_w2 = nrm(ks[14], (DEPTH, D_FF, D_MODEL), D_FF ** -0.5)
    final_norm = 1.0 + nrm(ks[15], (D_MODEL,), 0.02)
    return {"x": x,
            "a_w_in": a_w_in, "a_ln_g": a_ln_g, "a_ln_b": a_ln_b, "a_w_s": a_w_s,
            "a_b_s": a_b_s, "a_w_out": a_w_out,
            "b_w_in": b_w_in, "b_w_grp": b_w_grp, "b_scale": b_scale, "b_w_out": b_w_out,
            "norm_mix": norm_mix, "norm_mlp": norm_mlp, "mlp_w1": mlp_w1, "mlp_w2": mlp_w2,
            "final_norm": final_norm}


def reference(x, a_w_in, a_ln_g, a_ln_b, a_w_s, a_b_s, a_w_out,
              b_w_in, b_w_grp, b_scale, b_w_out,
              norm_mix, norm_mlp, mlp_w1, mlp_w2, final_norm):
    h = x
    for i in range(DEPTH):
        hn = rmsnorm(h, norm_mix[i])
        if i % N_MIXERS == 0:
            j = i // N_MIXERS
            mix = chunked_gmlp_mixer(hn, a_w_in[j], a_ln_g[j], a_ln_b[j],
                                     a_w_s[j], a_b_s[j], a_w_out[j])
        else:
            j = i // N_MIXERS
            mix = multiscale_pool_mixer(hn, b_w_in[j], b_w_grp[j], b_scale[j], b_w_out[j])
        h = h + mix
        h = h + sqrelu_mlp(rmsnorm(h, norm_mlp[i]), mlp_w1[i], mlp_w2[i])
    return rmsnorm(h, final_norm)
```

```python
import functools

import numpy as np
import jax
import jax.numpy as jnp
from jax import lax
from jax.experimental import pallas as pl
from jax.experimental.pallas import tpu as pltpu

EPS = 1e-6
CHUNK = 128
A_GROUPS = 8
B_WINDOWS = (2, 4, 8, 16)
HALO = 16

VMEM_LIMIT_BYTES = 56 * 1024 * 1024

F32 = jnp.float32
BF16 = jnp.bfloat16


def _rmsnorm(x, g):
    return x * lax.rsqrt(jnp.mean(x * x, axis=-1, keepdims=True) + EPS) * g


def _const_spec(shape):
    return pl.BlockSpec(shape, lambda *_: (0,) * len(shape), pipeline_mode=pl.Buffered(1))


def _gmlp_in_kernel(x_ref, g_ref, w_ref, z_ref, hn_ref):
    @pl.when(pl.program_id(1) == 0)
    def _():
        hn_ref[...] = _rmsnorm(x_ref[...], g_ref[...]).astype(BF16)

    z = jnp.dot(hn_ref[...], w_ref[...], preferred_element_type=F32)
    sqrt_half = np.float32(np.sqrt(0.5))
    z_ref[...] = (0.5 * z * (1.0 + lax.erf(z * sqrt_half))).astype(BF16)


def _gmlp_in(x, g, w, *, tm, tn):
    m, d = x.shape
    n = w.shape[1]
    return pl.pallas_call(
        _gmlp_in_kernel,
        grid=(m // tm, n // tn),
        in_specs=[
            pl.BlockSpec((tm, d), lambda i, j: (i, 0)),
            pl.BlockSpec((1, d), lambda i, j: (0, 0)),
            pl.BlockSpec((d, tn), lambda i, j: (0, j)),
        ],
        out_specs=pl.BlockSpec((tm, tn), lambda i, j: (i, j)),
        out_shape=jax.ShapeDtypeStruct((m, n), BF16),
        scratch_shapes=[pltpu.VMEM((tm, d), BF16)],
        compiler_params=pltpu.CompilerParams(
            dimension_semantics=("parallel", "arbitrary"),
            vmem_limit_bytes=VMEM_LIMIT_BYTES),
        name="gmlp_in",
    )(x, g, w)


def _gmlp_out_kernel(x_ref, u_ref, v_ref, lng_ref, lnb_ref, ws_ref, bst_ref, wout_ref,
                     o_ref, vn_ref, gated_ref):
    tm, width = vn_ref.shape
    head = width // A_GROUPS

    v = v_ref[...].astype(F32)
    mu = jnp.mean(v, axis=-1, keepdims=True)
    vc = v - mu
    var = jnp.mean(vc * vc, axis=-1, keepdims=True)
    vn_ref[...] = (vc * lax.rsqrt(var + EPS) * lng_ref[...] + lnb_ref[...]).astype(BF16)

    row = lax.broadcasted_iota(jnp.int32, (CHUNK, CHUNK), 0)
    col = lax.broadcasted_iota(jnp.int32, (CHUNK, CHUNK), 1)
    causal = (row >= col).astype(F32)
    for g in range(A_GROUPS):
        cols = slice(g * head, (g + 1) * head)
        w_g = (ws_ref[g] * causal).astype(BF16)
        b_g = bst_ref[:, g:g + 1]
        for c in range(tm // CHUNK):
            rows = slice(c * CHUNK, (c + 1) * CHUNK)
            s = jnp.dot(w_g, vn_ref[rows, cols], preferred_element_type=F32) + b_g
            gated_ref[rows, cols] = (u_ref[rows, cols].astype(F32) * s).astype(BF16)

    o_ref[...] = x_ref[...] + jnp.dot(gated_ref[...], wout_ref[...],
                                      preferred_element_type=F32)


def _gmlp_out(x, z, ln_g, ln_b, w_s, b_s_t, w_out, *, tm):
    m, d = x.shape
    width = w_out.shape[0]
    return pl.pallas_call(
        _gmlp_out_kernel,
        grid=(m // tm,),
        in_specs=[
            pl.BlockSpec((tm, d), lambda i: (i, 0)),
            pl.BlockSpec((tm, width), lambda i: (i, 0)),
            pl.BlockSpec((tm, width), lambda i: (i, 1)),
            _const_spec((1, width)),
            _const_spec((1, width)),
            _const_spec(w_s.shape),
            _const_spec(b_s_t.shape),
            _const_spec(w_out.shape),
        ],
        out_specs=pl.BlockSpec((tm, d), lambda i: (i, 0)),
        out_shape=jax.ShapeDtypeStruct((m, d), F32),
        scratch_shapes=[pltpu.VMEM((tm, width), BF16), pltpu.VMEM((tm, width), BF16)],
        compiler_params=pltpu.CompilerParams(
            dimension_semantics=("parallel",),
            vmem_limit_bytes=VMEM_LIMIT_BYTES),
        name="gmlp_out",
    )(x, z, z, ln_g, ln_b, w_s, b_s_t, w_out)


def _mlp_kernel(h_ref, g_ref, w1_ref, w2_ref, fg_ref, o_ref, hn_ref, *, apply_final_norm):
    f = pl.program_id(1)

    @pl.when(f == 0)
    def _():
        h = h_ref[...]
        hn_ref[...] = _rmsnorm(h, g_ref[...]).astype(BF16)
        o_ref[...] = h

    a = jnp.dot(hn_ref[...], w1_ref[...], preferred_element_type=F32)
    a = jnp.maximum(a, 0.0)
    o_ref[...] += jnp.dot((a * a).astype(BF16), w2_ref[...], preferred_element_type=F32)

    if apply_final_norm:
        @pl.when(f == pl.num_programs(1) - 1)
        def _():
            o_ref[...] = _rmsnorm(o_ref[...], fg_ref[...])


def _mlp(h, g, w1, w2, final_g, *, tm, tf, apply_final_norm):
    m, d = h.shape
    ff = w1.shape[1]
    return pl.pallas_call(
        functools.partial(_mlp_kernel, apply_final_norm=apply_final_norm),
        grid=(m // tm, ff // tf),
        in_specs=[
            pl.BlockSpec((tm, d), lambda i, j: (i, 0)),
            pl.BlockSpec((1, d), lambda i, j: (0, 0)),
            pl.BlockSpec((d, tf), lambda i, j: (0, j)),
            pl.BlockSpec((tf, d), lambda i, j: (j, 0)),
            pl.BlockSpec((1, d), lambda i, j: (0, 0)),
        ],
        out_specs=pl.BlockSpec((tm, d), lambda i, j: (i, 0)),
        out_shape=jax.ShapeDtypeStruct((m, d), F32),
        scratch_shapes=[pltpu.VMEM((tm, d), BF16)],
        compiler_params=pltpu.CompilerParams(
            dimension_semantics=("parallel", "arbitrary"),
            vmem_limit_bytes=VMEM_LIMIT_BYTES),
        name="mlp_final" if apply_final_norm else "mlp",
    )(h, g, w1, w2, final_g)


def _pool_kernel(x_ref, g_ref, win_ref, wgrp_ref, scale_ref, wout_ref, o_ref,
                 vbuf_ref, mixed_ref, *, seq):
    tm, width = mixed_ref.shape
    head = width // len(B_WINDOWS)
    i = pl.program_id(0)
    start = (i * tm) % seq

    @pl.when(start == 0)
    def _():
        vbuf_ref[0:HALO, :] = jnp.zeros((HALO, width), F32)

    @pl.when(start != 0)
    def _():
        vbuf_ref[0:HALO, :] = vbuf_ref[tm:tm + HALO, :]

    x = x_ref[...]
    hn = _rmsnorm(x, g_ref[...]).astype(BF16)
    vbuf_ref[HALO:HALO + tm, :] = jnp.dot(hn, win_ref[...], preferred_element_type=F32)

    pos = start + lax.broadcasted_iota(jnp.int32, (tm, 1), 0)
    for g, w in enumerate(B_WINDOWS):
        cols = slice(g * head, (g + 1) * head)
        v = vbuf_ref[HALO:HALO + tm, cols]
        total = v
        for j in range(1, w):
            total = total + vbuf_ref[HALO - j:HALO - j + tm, cols]
        inv_count = 1.0 / jnp.minimum(pos + 1, w).astype(F32)
        pooled = (total * inv_count - v).astype(BF16)
        mixed = jnp.dot(pooled, wgrp_ref[g], preferred_element_type=F32)
        mixed_ref[:, cols] = (mixed * scale_ref[:, cols]).astype(BF16)

    o_ref[...] = x + jnp.dot(mixed_ref[...], wout_ref[...], preferred_element_type=F32)


def _pool_mixer(x, g, w_in, w_grp, scale, w_out, *, tm, seq):
    m, d = x.shape
    width = w_in.shape[1]
    return pl.pallas_call(
        functools.partial(_pool_kernel, seq=seq),
        grid=(m // tm,),
        in_specs=[
            pl.BlockSpec((tm, d), lambda i: (i, 0)),
            _const_spec((1, d)),
            _const_spec(w_in.shape),
            _const_spec(w_grp.shape),
            _const_spec((1, width)),
            _const_spec(w_out.shape),
        ],
        out_specs=pl.BlockSpec((tm, d), lambda i: (i, 0)),
        out_shape=jax.ShapeDtypeStruct((m, d), F32),
        scratch_shapes=[pltpu.VMEM((HALO + tm, width), F32), pltpu.VMEM((tm, width), BF16)],
        compiler_params=pltpu.CompilerParams(
            dimension_semantics=("arbitrary",),
            vmem_limit_bytes=VMEM_LIMIT_BYTES),
        name="pool_mixer",
    )(x, g, w_in, w_grp, scale, w_out)


def kernel(x, a_w_in, a_ln_g, a_ln_b, a_w_s, a_b_s, a_w_out, b_w_in, b_w_grp, b_scale, b_w_out,
           norm_mix, norm_mlp, mlp_w1, mlp_w2, final_norm):
    bsz, seq, d = x.shape
    depth = norm_mix.shape[0]
    h = x.reshape(bsz * seq, d)
    row = lambda p: p.reshape(1, -1)
    for i in range(depth):
        j = i // 2
        if i % 2 == 0:
            z = _gmlp_in(h, row(norm_mix[i]), a_w_in[j].astype(BF16), tm=1024, tn=1024)
            h = _gmlp_out(h, z, row(a_ln_g[j]), row(a_ln_b[j]), a_w_s[j], a_b_s[j].T,
                          a_w_out[j].astype(BF16), tm=512)
        else:
            h = _pool_mixer(h, row(norm_mix[i]), b_w_in[j].astype(BF16),
                            b_w_grp[j].astype(BF16), row(b_scale[j]),
                            b_w_out[j].astype(BF16), tm=512, seq=seq)
        h = _mlp(h, row(norm_mlp[i]), mlp_w1[i].astype(BF16), mlp_w2[i].astype(BF16),
                 row(final_norm), tm=1024, tf=512, apply_final_norm=(i == depth - 1))
    return h.reshape(bsz, seq, d)
```

```python
import functools

import numpy as np
import jax
import jax.numpy as jnp
from jax import lax
from jax.experimental import pallas as pl
from jax.experimental.pallas import tpu as pltpu

EPS = 1e-6
CHUNK = 128
A_GROUPS = 8
B_WINDOWS = (2, 4, 8, 16)
HALO = 16

VMEM_LIMIT_BYTES = 60 * 1024 * 1024

F32 = jnp.float32
BF16 = jnp.bfloat16


def _rmsnorm(x, g):
    return x * lax.rsqrt(jnp.mean(x * x, axis=-1, keepdims=True) + EPS) * g


def _const_spec(shape):
    return pl.BlockSpec(shape, lambda *_: (0,) * len(shape), pipeline_mode=pl.Buffered(1))


def _gmlp_in_kernel(x_ref, g_ref, w_ref, z_ref, hn_ref):
    @pl.when(pl.program_id(1) == 0)
    def _():
        hn_ref[...] = _rmsnorm(x_ref[...], g_ref[...]).astype(BF16)

    z = jnp.dot(hn_ref[...], w_ref[...].astype(BF16), preferred_element_type=F32)
    sqrt_half = np.float32(np.sqrt(0.5))
    z_ref[...] = (0.5 * z * (1.0 + lax.erf(z * sqrt_half))).astype(BF16)


def _gmlp_in(x, g, w, layer, *, tm, tn):
    m, d = x.shape
    n = w.shape[2]
    return pl.pallas_call(
        _gmlp_in_kernel,
        grid=(m // tm, n // tn),
        in_specs=[
            pl.BlockSpec((tm, d), lambda i, j: (i, 0)),
            pl.BlockSpec((1, d), lambda i, j: (0, 0)),
            pl.BlockSpec((None, d, tn), lambda i, j: (layer, 0, j)),
        ],
        out_specs=pl.BlockSpec((tm, tn), lambda i, j: (i, j)),
        out_shape=jax.ShapeDtypeStruct((m, n), BF16),
        scratch_shapes=[pltpu.VMEM((tm, d), BF16)],
        compiler_params=pltpu.CompilerParams(
            dimension_semantics=("parallel", "arbitrary"),
            vmem_limit_bytes=VMEM_LIMIT_BYTES),
        name="gmlp_in",
    )(x, g, w)


def _gmlp_out_kernel(x_ref, u_ref, v_ref, lng_ref, lnb_ref, ws_ref, bst_ref, wout_ref,
                     o_ref, vn_ref, gated_ref):
    tm, width = vn_ref.shape
    head = width // A_GROUPS

    v = v_ref[...].astype(F32)
    mu = jnp.mean(v, axis=-1, keepdims=True)
    vc = v - mu
    var = jnp.mean(vc * vc, axis=-1, keepdims=True)
    vn_ref[...] = (vc * lax.rsqrt(var + EPS) * lng_ref[...] + lnb_ref[...]).astype(BF16)

    row = lax.broadcasted_iota(jnp.int32, (CHUNK, CHUNK), 0)
    col = lax.broadcasted_iota(jnp.int32, (CHUNK, CHUNK), 1)
    causal = (row >= col).astype(F32)
    for g in range(A_GROUPS):
        cols = slice(g * head, (g + 1) * head)
        w_g = (ws_ref[g] * causal).astype(BF16)
        b_g = bst_ref[:, g:g + 1]
        for c in range(tm // CHUNK):
            rows = slice(c * CHUNK, (c + 1) * CHUNK)
            s = jnp.dot(w_g, vn_ref[rows, cols], preferred_element_type=F32) + b_g
            gated_ref[rows, cols] = (u_ref[rows, cols].astype(F32) * s).astype(BF16)

    o_ref[...] = x_ref[...] + jnp.dot(gated_ref[...], wout_ref[...],
                                      preferred_element_type=F32)


def _gmlp_out(x, z, ln_g, ln_b, w_s, b_s_t, w_out, *, tm):
    m, d = x.shape
    width = w_out.shape[0]
    return pl.pallas_call(
        _gmlp_out_kernel,
        grid=(m // tm,),
        in_specs=[
            pl.BlockSpec((tm, d), lambda i: (i, 0)),
            pl.BlockSpec((tm, width), lambda i: (i, 0)),
            pl.BlockSpec((tm, width), lambda i: (i, 1)),
            _const_spec((1, width)),
            _const_spec((1, width)),
            _const_spec(w_s.shape),
            _const_spec(b_s_t.shape),
            _const_spec(w_out.shape),
        ],
        out_specs=pl.BlockSpec((tm, d), lambda i: (i, 0)),
        out_shape=jax.ShapeDtypeStruct((m, d), F32),
        scratch_shapes=[pltpu.VMEM((tm, width), BF16), pltpu.VMEM((tm, width), BF16)],
        compiler_params=pltpu.CompilerParams(
            dimension_semantics=("parallel",),
            vmem_limit_bytes=VMEM_LIMIT_BYTES),
        name="gmlp_out",
    )(x, z, z, ln_g, ln_b, w_s, b_s_t, w_out)


def _mlp_kernel(h_ref, g_ref, w1_ref, w2_ref, fg_ref, o_ref, hn_ref, *, apply_final_norm):
    f = pl.program_id(1)

    @pl.when(f == 0)
    def _():
        h = h_ref[...]
        hn_ref[...] = _rmsnorm(h, g_ref[...]).astype(BF16)
        o_ref[...] = h

    a = jnp.dot(hn_ref[...], w1_ref[...].astype(BF16), preferred_element_type=F32)
    a = jnp.maximum(a, 0.0)
    o_ref[...] += jnp.dot((a * a).astype(BF16), w2_ref[...].astype(BF16),
                          preferred_element_type=F32)

    if apply_final_norm:
        @pl.when(f == pl.num_programs(1) - 1)
        def _():
            o_ref[...] = _rmsnorm(o_ref[...], fg_ref[...])


def _mlp(h, g, w1, w2, final_g, layer, *, tm, tf, apply_final_norm):
    m, d = h.shape
    ff = w1.shape[2]
    return pl.pallas_call(
        functools.partial(_mlp_kernel, apply_final_norm=apply_final_norm),
        grid=(m // tm, ff // tf),
        in_specs=[
            pl.BlockSpec((tm, d), lambda i, j: (i, 0)),
            pl.BlockSpec((1, d), lambda i, j: (0, 0)),
            pl.BlockSpec((None, d, tf), lambda i, j: (layer, 0, j)),
            pl.BlockSpec((None, tf, d), lambda i, j: (layer, j, 0)),
            pl.BlockSpec((1, d), lambda i, j: (0, 0)),
        ],
        out_specs=pl.BlockSpec((tm, d), lambda i, j: (i, 0)),
        out_shape=jax.ShapeDtypeStruct((m, d), F32),
        scratch_shapes=[pltpu.VMEM((tm, d), BF16)],
        compiler_params=pltpu.CompilerParams(
            dimension_semantics=("parallel", "arbitrary"),
            vmem_limit_bytes=VMEM_LIMIT_BYTES),
        name="mlp_final" if apply_final_norm else "mlp",
    )(h, g, w1, w2, final_g)


def _pool_kernel(x_ref, g_ref, win_ref, wgrp_ref, scale_ref, wout_ref, o_ref,
                 vbuf_ref, mixed_ref, *, seq):
    tm, width = mixed_ref.shape
    head = width // len(B_WINDOWS)
    i = pl.program_id(0)
    start = (i * tm) % seq

    @pl.when(start == 0)
    def _():
        vbuf_ref[0:HALO, :] = jnp.zeros((HALO, width), F32)

    @pl.when(start != 0)
    def _():
        vbuf_ref[0:HALO, :] = vbuf_ref[tm:tm + HALO, :]

    x = x_ref[...]
    hn = _rmsnorm(x, g_ref[...]).astype(BF16)
    vbuf_ref[HALO:HALO + tm, :] = jnp.dot(hn, win_ref[...], preferred_element_type=F32)

    pos = start + lax.broadcasted_iota(jnp.int32, (tm, 1), 0)
    for g, w in enumerate(B_WINDOWS):
        cols = slice(g * head, (g + 1) * head)
        v = vbuf_ref[HALO:HALO + tm, cols]
        total = v
        for j in range(1, w):
            total = total + vbuf_ref[HALO - j:HALO - j + tm, cols]
        inv_count = 1.0 / jnp.minimum(pos + 1, w).astype(F32)
        pooled = (total * inv_count - v).astype(BF16)
        mixed = jnp.dot(pooled, wgrp_ref[g], preferred_element_type=F32)
        mixed_ref[:, cols] = (mixed * scale_ref[:, cols]).astype(BF16)

    o_ref[...] = x + jnp.dot(mixed_ref[...], wout_ref[...], preferred_element_type=F32)


def _pool_mixer(x, g, w_in, w_grp, scale, w_out, *, tm, seq):
    m, d = x.shape
    width = w_in.shape[1]
    return pl.pallas_call(
        functools.partial(_pool_kernel, seq=seq),
        grid=(m // tm,),
        in_specs=[
            pl.BlockSpec((tm, d), lambda i: (i, 0)),
            _const_spec((1, d)),
            _const_spec(w_in.shape),
            _const_spec(w_grp.shape),
            _const_spec((1, width)),
            _const_spec(w_out.shape),
        ],
        out_specs=pl.BlockSpec((tm, d), lambda i: (i, 0)),
        out_shape=jax.ShapeDtypeStruct((m, d), F32),
        scratch_shapes=[pltpu.VMEM((HALO + tm, width), F32), pltpu.VMEM((tm, width), BF16)],
        compiler_params=pltpu.CompilerParams(
            dimension_semantics=("arbitrary",),
            vmem_limit_bytes=VMEM_LIMIT_BYTES),
        name="pool_mixer",
    )(x, g, w_in, w_grp, scale, w_out)


def kernel(x, a_w_in, a_ln_g, a_ln_b, a_w_s, a_b_s, a_w_out, b_w_in, b_w_grp, b_scale, b_w_out,
           norm_mix, norm_mlp, mlp_w1, mlp_w2, final_norm):
    bsz, seq, d = x.shape
    depth = norm_mix.shape[0]
    h = x.reshape(bsz * seq, d)
    row = lambda p: p.reshape(1, -1)
    for i in range(depth):
        j = i // 2
        if i % 2 == 0:
            z = _gmlp_in(h, row(norm_mix[i]), a_w_in, j, tm=1024, tn=1024)
            h = _gmlp_out(h, z, row(a_ln_g[j]), row(a_ln_b[j]), a_w_s[j], a_b_s[j].T,
                          a_w_out[j].astype(BF16), tm=512)
        else:
            h = _pool_mixer(h, row(norm_mix[i]), b_w_in[j].astype(BF16),
                            b_w_grp[j].astype(BF16), row(b_scale[j]),
                            b_w_out[j].astype(BF16), tm=512, seq=seq)
        h = _mlp(h, row(norm_mlp[i]), mlp_w1, mlp_w2, row(final_norm), i,
                 tm=1024, tf=512, apply_final_norm=(i == depth - 1))
    return h.reshape(bsz, seq, d)
```

```python
import functools

import numpy as np
import jax
import jax.numpy as jnp
from jax import lax
from jax.experimental import pallas as pl
from jax.experimental.pallas import tpu as pltpu

EPS = 1e-6
CHUNK = 128
A_GROUPS = 8
B_WINDOWS = (2, 4, 8, 16)
HALO = 16

VMEM_LIMIT_BYTES = 60 * 1024 * 1024

F32 = jnp.float32
BF16 = jnp.bfloat16


def _rmsnorm(x, g):
    return x * lax.rsqrt(jnp.mean(x * x, axis=-1, keepdims=True) + EPS) * g


def _const_spec(shape):
    return pl.BlockSpec(shape, lambda *_: (0,) * len(shape), pipeline_mode=pl.Buffered(1))


def _gmlp_in_kernel(x_ref, g_ref, w_ref, z_ref, hn_ref):
    @pl.when(pl.program_id(1) == 0)
    def _():
        hn_ref[...] = _rmsnorm(x_ref[...], g_ref[...]).astype(BF16)

    z = jnp.dot(hn_ref[...], w_ref[...].astype(BF16), preferred_element_type=F32)
    sqrt_half = np.float32(np.sqrt(0.5))
    z_ref[...] = (0.5 * z * (1.0 + lax.erf(z * sqrt_half))).astype(BF16)


def _gmlp_in(x, g, w, layer, *, tm, tn):
    m, d = x.shape
    n = w.shape[2]
    return pl.pallas_call(
        _gmlp_in_kernel,
        grid=(m // tm, n // tn),
        in_specs=[
            pl.BlockSpec((tm, d), lambda i, j: (i, 0)),
            pl.BlockSpec((1, d), lambda i, j: (0, 0)),
            pl.BlockSpec((None, d, tn), lambda i, j: (layer, 0, j)),
        ],
        out_specs=pl.BlockSpec((tm, tn), lambda i, j: (i, j)),
        out_shape=jax.ShapeDtypeStruct((m, n), BF16),
        scratch_shapes=[pltpu.VMEM((tm, d), BF16)],
        compiler_params=pltpu.CompilerParams(
            dimension_semantics=("parallel", "arbitrary"),
            vmem_limit_bytes=VMEM_LIMIT_BYTES),
        name="gmlp_in",
    )(x, g, w)


def _gmlp_out_kernel(x_ref, u_ref, v_ref, lng_ref, lnb_ref, ws_ref, bst_ref, wout_ref,
                     o_ref, vn_ref, gated_ref):
    tm, width = vn_ref.shape
    head = width // A_GROUPS

    v = v_ref[...].astype(F32)
    mu = jnp.mean(v, axis=-1, keepdims=True)
    vc = v - mu
    var = jnp.mean(vc * vc, axis=-1, keepdims=True)
    vn_ref[...] = (vc * lax.rsqrt(var + EPS) * lng_ref[...] + lnb_ref[...]).astype(BF16)

    row = lax.broadcasted_iota(jnp.int32, (CHUNK, CHUNK), 0)
    col = lax.broadcasted_iota(jnp.int32, (CHUNK, CHUNK), 1)
    causal = (row >= col).astype(F32)
    for g in range(A_GROUPS):
        cols = slice(g * head, (g + 1) * head)
        w_g = (ws_ref[g] * causal).astype(BF16)
        b_g = bst_ref[:, g:g + 1]
        for c in range(tm // CHUNK):
            rows = slice(c * CHUNK, (c + 1) * CHUNK)
            s = jnp.dot(w_g, vn_ref[rows, cols], preferred_element_type=F32) + b_g
            gated_ref[rows, cols] = (u_ref[rows, cols].astype(F32) * s).astype(BF16)

    o_ref[...] = x_ref[...] + jnp.dot(gated_ref[...], wout_ref[...],
                                      preferred_element_type=F32)


def _gmlp_out(x, z, ln_g, ln_b, w_s, b_s_t, w_out, *, tm):
    m, d = x.shape
    width = w_out.shape[0]
    return pl.pallas_call(
        _gmlp_out_kernel,
        grid=(m // tm,),
        in_specs=[
            pl.BlockSpec((tm, d), lambda i: (i, 0)),
            pl.BlockSpec((tm, width), lambda i: (i, 0)),
            pl.BlockSpec((tm, width), lambda i: (i, 1)),
            _const_spec((1, width)),
            _const_spec((1, width)),
            _const_spec(w_s.shape),
            _const_spec(b_s_t.shape),
            _const_spec(w_out.shape),
        ],
        out_specs=pl.BlockSpec((tm, d), lambda i: (i, 0)),
        out_shape=jax.ShapeDtypeStruct((m, d), F32),
        scratch_shapes=[pltpu.VMEM((tm, width), BF16), pltpu.VMEM((tm, width), BF16)],
        compiler_params=pltpu.CompilerParams(
            dimension_semantics=("parallel",),
            vmem_limit_bytes=VMEM_LIMIT_BYTES),
        name="gmlp_out",
    )(x, z, z, ln_g, ln_b, w_s, b_s_t, w_out)


def _mlp_kernel(h_ref, g_ref, w1_hbm, w2_hbm, fg_ref, o_ref, hn_ref, w1_buf, w2_buf, sem,
                *, layer, tf, apply_final_norm):
    i = pl.program_id(0)
    n_tiles = w1_hbm.shape[2] // tf
    assert n_tiles % 2 == 0

    def copies(t, slot):
        off = t * tf if isinstance(t, int) else pl.multiple_of(t * tf, tf)
        return (
            pltpu.make_async_copy(w1_hbm.at[layer, :, pl.ds(off, tf)], w1_buf.at[slot],
                                  sem.at[0, slot]),
            pltpu.make_async_copy(w2_hbm.at[layer, pl.ds(off, tf), :], w2_buf.at[slot],
                                  sem.at[1, slot]),
        )

    def start(t, slot):
        for c in copies(t, slot):
            c.start()

    def wait(t, slot):
        for c in copies(t, slot):
            c.wait()

    @pl.when(i == 0)
    def _():
        start(0, 0)

    h = h_ref[...]
    hn_ref[...] = _rmsnorm(h, g_ref[...]).astype(BF16)
    o_ref[...] = h

    def accumulate(slot):
        a = jnp.dot(hn_ref[...], w1_buf[slot].astype(BF16), preferred_element_type=F32)
        a = jnp.maximum(a, 0.0)
        o_ref[...] += jnp.dot((a * a).astype(BF16), w2_buf[slot].astype(BF16),
                              preferred_element_type=F32)

    def pair(p, carry):
        t = 2 * p
        start(t + 1, 1)
        wait(t, 0)
        accumulate(0)

        is_last_tile = t + 2 == n_tiles

        @pl.when(jnp.logical_not(jnp.logical_and(is_last_tile, i == pl.num_programs(0) - 1)))
        def _():
            start(jnp.where(is_last_tile, 0, t + 2), 0)

        wait(t + 1, 1)
        accumulate(1)
        return carry

    lax.fori_loop(0, n_tiles // 2, pair, 0)

    if apply_final_norm:
        o_ref[...] = _rmsnorm(o_ref[...], fg_ref[...])


def _mlp(h, g, w1, w2, final_g, layer, *, tm, tf, apply_final_norm):
    m, d = h.shape
    return pl.pallas_call(
        functools.partial(_mlp_kernel, layer=layer, tf=tf, apply_final_norm=apply_final_norm),
        grid=(m // tm,),
        in_specs=[
            pl.BlockSpec((tm, d), lambda i: (i, 0)),
            pl.BlockSpec((1, d), lambda i: (0, 0)),
            pl.BlockSpec(memory_space=pl.ANY),
            pl.BlockSpec(memory_space=pl.ANY),
            pl.BlockSpec((1, d), lambda i: (0, 0)),
        ],
        out_specs=pl.BlockSpec((tm, d), lambda i: (i, 0)),
        out_shape=jax.ShapeDtypeStruct((m, d), F32),
        scratch_shapes=[
            pltpu.VMEM((tm, d), BF16),
            pltpu.VMEM((2, d, tf), F32),
            pltpu.VMEM((2, tf, d), F32),
            pltpu.SemaphoreType.DMA((2, 2)),
        ],
        compiler_params=pltpu.CompilerParams(
            dimension_semantics=("arbitrary",),
            vmem_limit_bytes=VMEM_LIMIT_BYTES),
        name="mlp_final" if apply_final_norm else "mlp",
    )(h, g, w1, w2, final_g)


def _pool_kernel(x_ref, g_ref, win_ref, wgrp_ref, scale_ref, wout_ref, o_ref,
                 vbuf_ref, mixed_ref, *, seq):
    tm, width = mixed_ref.shape
    head = width // len(B_WINDOWS)
    i = pl.program_id(0)
    start = (i * tm) % seq

    @pl.when(start == 0)
    def _():
        vbuf_ref[0:HALO, :] = jnp.zeros((HALO, width), F32)

    @pl.when(start != 0)
    def _():
        vbuf_ref[0:HALO, :] = vbuf_ref[tm:tm + HALO, :]

    x = x_ref[...]
    hn = _rmsnorm(x, g_ref[...]).astype(BF16)
    vbuf_ref[HALO:HALO + tm, :] = jnp.dot(hn, win_ref[...], preferred_element_type=F32)

    pos = start + lax.broadcasted_iota(jnp.int32, (tm, 1), 0)
    for g, w in enumerate(B_WINDOWS):
        cols = slice(g * head, (g + 1) * head)
        v = vbuf_ref[HALO:HALO + tm, cols]
        total = v
        for j in range(1, w):
            total = total + vbuf_ref[HALO - j:HALO - j + tm, cols]
        inv_count = 1.0 / jnp.minimum(pos + 1, w).astype(F32)
        pooled = (total * inv_count - v).astype(BF16)
        mixed = jnp.dot(pooled, wgrp_ref[g], preferred_element_type=F32)
        mixed_ref[:, cols] = (mixed * scale_ref[:, cols]).astype(BF16)

    o_ref[...] = x + jnp.dot(mixed_ref[...], wout_ref[...], preferred_element_type=F32)


def _pool_mixer(x, g, w_in, w_grp, scale, w_out, *, tm, seq):
    m, d = x.shape
    width = w_in.shape[1]
    return pl.pallas_call(
        functools.partial(_pool_kernel, seq=seq),
        grid=(m // tm,),
        in_specs=[
            pl.BlockSpec((tm, d), lambda i: (i, 0)),
            _const_spec((1, d)),
            _const_spec(w_in.shape),
            _const_spec(w_grp.shape),
            _const_spec((1, width)),
            _const_spec(w_out.shape),
        ],
        out_specs=pl.BlockSpec((tm, d), lambda i: (i, 0)),
        out_shape=jax.ShapeDtypeStruct((m, d), F32),
        scratch_shapes=[pltpu.VMEM((HALO + tm, width), F32), pltpu.VMEM((tm, width), BF16)],
        compiler_params=pltpu.CompilerParams(
            dimension_semantics=("arbitrary",),
            vmem_limit_bytes=VMEM_LIMIT_BYTES),
        name="pool_mixer",
    )(x, g, w_in, w_grp, scale, w_out)


def kernel(x, a_w_in, a_ln_g, a_ln_b, a_w_s, a_b_s, a_w_out, b_w_in, b_w_grp, b_scale, b_w_out,
           norm_mix, norm_mlp, mlp_w1, mlp_w2, final_norm):
    bsz, seq, d = x.shape
    depth = norm_mix.shape[0]
    h = x.reshape(bsz * seq, d)
    row = lambda p: p.reshape(1, -1)
    for i in range(depth):
        j = i // 2
        if i % 2 == 0:
            z = _gmlp_in(h, row(norm_mix[i]), a_w_in, j, tm=1024, tn=1024)
            h = _gmlp_out(h, z, row(a_ln_g[j]), row(a_ln_b[j]), a_w_s[j], a_b_s[j].T,
                          a_w_out[j].astype(BF16), tm=512)
        else:
            h = _pool_mixer(h, row(norm_mix[i]), b_w_in[j].astype(BF16),
                            b_w_grp[j].astype(BF16), row(b_scale[j]),
                            b_w_out[j].astype(BF16), tm=512, seq=seq)
        h = _mlp(h, row(norm_mlp[i]), mlp_w1, mlp_w2, row(final_norm), i,
                 tm=1024, tf=512, apply_final_norm=(i == depth - 1))
    return h.reshape(bsz, seq, d)
```

```python
import functools

import numpy as np
import jax
import jax.numpy as jnp
from jax import lax
from jax.experimental import pallas as pl
from jax.experimental.pallas import tpu as pltpu

EPS = 1e-6
CHUNK = 128
A_GROUPS = 8
B_WINDOWS = (2, 4, 8, 16)
HALO = 16

VMEM_LIMIT_BYTES = 60 * 1024 * 1024

F32 = jnp.float32
BF16 = jnp.bfloat16


def _rmsnorm(x, g):
    return x * lax.rsqrt(jnp.mean(x * x, axis=-1, keepdims=True) + EPS) * g


def _const_spec(shape):
    return pl.BlockSpec(shape, lambda *_: (0,) * len(shape), pipeline_mode=pl.Buffered(1))


def _gmlp_in_kernel(x_ref, g_ref, w_hbm, z_ref, hn_ref, w_buf, sem, *, layer, tn):
    i = pl.program_id(0)
    n_tiles = w_hbm.shape[2] // tn

    n_parts = sem.shape[1]
    rows = w_hbm.shape[1] // n_parts

    def copies(t, slot):
        return [
            pltpu.make_async_copy(
                w_hbm.at[layer, p * rows:(p + 1) * rows, t * tn:(t + 1) * tn],
                w_buf.at[slot, p * rows:(p + 1) * rows, :], sem.at[slot, p])
            for p in range(n_parts)]

    def start(t, slot):
        for c in copies(t, slot):
            c.start()

    @pl.when(i == 0)
    def _():
        start(0, 0)

    hn_ref[...] = _rmsnorm(x_ref[...], g_ref[...]).astype(BF16)

    sqrt_half = np.float32(np.sqrt(0.5))
    for t in range(n_tiles):
        slot = t % 2
        if t + 1 < n_tiles:
            start(t + 1, 1 - slot)
        else:
            @pl.when(i != pl.num_programs(0) - 1)
            def _():
                start(0, 1 - slot)
        for c in copies(t, slot):
            c.wait()
        z = jnp.dot(hn_ref[...], w_buf[slot].astype(BF16), preferred_element_type=F32)
        z_ref[:, t * tn:(t + 1) * tn] = (0.5 * z * (1.0 + lax.erf(z * sqrt_half))).astype(BF16)


def _gmlp_in(x, g, w, layer, *, tm, tn):
    m, d = x.shape
    n = w.shape[2]
    assert (n // tn) % 2 == 0
    return pl.pallas_call(
        functools.partial(_gmlp_in_kernel, layer=layer, tn=tn),
        grid=(m // tm,),
        in_specs=[
            pl.BlockSpec((tm, d), lambda i: (i, 0)),
            pl.BlockSpec((1, d), lambda i: (0, 0)),
            pl.BlockSpec(memory_space=pl.ANY),
        ],
        out_specs=pl.BlockSpec((tm, n), lambda i: (i, 0)),
        out_shape=jax.ShapeDtypeStruct((m, n), BF16),
        scratch_shapes=[
            pltpu.VMEM((tm, d), BF16),
            pltpu.VMEM((2, d, tn), F32),
            pltpu.SemaphoreType.DMA((2, 4)),
        ],
        compiler_params=pltpu.CompilerParams(
            dimension_semantics=("arbitrary",),
            vmem_limit_bytes=VMEM_LIMIT_BYTES),
        name="gmlp_in",
    )(x, g, w)


def _gmlp_out_kernel(x_ref, u_ref, v_ref, lng_ref, lnb_ref, ws_ref, bst_ref, wout_ref,
                     o_ref, vn_ref, gated_ref):
    tm, width = vn_ref.shape
    head = width // A_GROUPS

    v = v_ref[...].astype(F32)
    mu = jnp.mean(v, axis=-1, keepdims=True)
    vc = v - mu
    var = jnp.mean(vc * vc, axis=-1, keepdims=True)
    vn_ref[...] = (vc * lax.rsqrt(var + EPS) * lng_ref[...] + lnb_ref[...]).astype(BF16)

    row = lax.broadcasted_iota(jnp.int32, (CHUNK, CHUNK), 0)
    col = lax.broadcasted_iota(jnp.int32, (CHUNK, CHUNK), 1)
    causal = (row >= col).astype(F32)
    for g in range(A_GROUPS):
        cols = slice(g * head, (g + 1) * head)
        w_g = (ws_ref[g] * causal).astype(BF16)
        b_g = bst_ref[:, g:g + 1]
        for c in range(tm // CHUNK):
            rows = slice(c * CHUNK, (c + 1) * CHUNK)
            s = jnp.dot(w_g, vn_ref[rows, cols], preferred_element_type=F32) + b_g
            gated_ref[rows, cols] = (u_ref[rows, cols].astype(F32) * s).astype(BF16)

    o_ref[...] = x_ref[...] + jnp.dot(gated_ref[...], wout_ref[...].astype(BF16),
                                      preferred_element_type=F32)


def _gmlp_out(x, z, ln_g, ln_b, w_s, b_s_t, w_out, *, tm):
    m, d = x.shape
    width = w_out.shape[0]
    return pl.pallas_call(
        _gmlp_out_kernel,
        grid=(m // tm,),
        in_specs=[
            pl.BlockSpec((tm, d), lambda i: (i, 0)),
            pl.BlockSpec((tm, width), lambda i: (i, 0)),
            pl.BlockSpec((tm, width), lambda i: (i, 1)),
            _const_spec((1, width)),
            _const_spec((1, width)),
            _const_spec(w_s.shape),
            _const_spec(b_s_t.shape),
            _const_spec(w_out.shape),
        ],
        out_specs=pl.BlockSpec((tm, d), lambda i: (i, 0)),
        out_shape=jax.ShapeDtypeStruct((m, d), F32),
        scratch_shapes=[pltpu.VMEM((tm, width), BF16), pltpu.VMEM((tm, width), BF16)],
        compiler_params=pltpu.CompilerParams(
            dimension_semantics=("parallel",),
            vmem_limit_bytes=VMEM_LIMIT_BYTES),
        name="gmlp_out",
    )(x, z, z, ln_g, ln_b, w_s, b_s_t, w_out)


def _mlp_kernel(h_ref, g_ref, w1_hbm, w2_hbm, fg_ref, o_ref, hn_ref, w1_buf, w2_buf, sem,
                *, layer, tf, apply_final_norm):
    i = pl.program_id(0)
    n_tiles = w1_hbm.shape[2] // tf
    assert n_tiles % 2 == 0

    def copies(t, slot):
        off = t * tf if isinstance(t, int) else pl.multiple_of(t * tf, tf)
        return (
            pltpu.make_async_copy(w1_hbm.at[layer, :, pl.ds(off, tf)], w1_buf.at[slot],
                                  sem.at[0, slot]),
            pltpu.make_async_copy(w2_hbm.at[layer, pl.ds(off, tf), :], w2_buf.at[slot],
                                  sem.at[1, slot]),
        )

    def start(t, slot):
        for c in copies(t, slot):
            c.start()

    def wait(t, slot):
        for c in copies(t, slot):
            c.wait()

    @pl.when(i == 0)
    def _():
        start(0, 0)

    h = h_ref[...]
    hn_ref[...] = _rmsnorm(h, g_ref[...]).astype(BF16)
    o_ref[...] = h

    def accumulate(slot):
        a = jnp.dot(hn_ref[...], w1_buf[slot].astype(BF16), preferred_element_type=F32)
        a = jnp.maximum(a, 0.0)
        o_ref[...] += jnp.dot((a * a).astype(BF16), w2_buf[slot].astype(BF16),
                              preferred_element_type=F32)

    def pair(p, carry):
        t = 2 * p
        start(t + 1, 1)
        wait(t, 0)
        accumulate(0)

        is_last_tile = t + 2 == n_tiles

        @pl.when(jnp.logical_not(jnp.logical_and(is_last_tile, i == pl.num_programs(0) - 1)))
        def _():
            start(jnp.where(is_last_tile, 0, t + 2), 0)

        wait(t + 1, 1)
        accumulate(1)
        return carry

    lax.fori_loop(0, n_tiles // 2, pair, 0)

    if apply_final_norm:
        o_ref[...] = _rmsnorm(o_ref[...], fg_ref[...])


def _mlp(h, g, w1, w2, final_g, layer, *, tm, tf, apply_final_norm):
    m, d = h.shape
    return pl.pallas_call(
        functools.partial(_mlp_kernel, layer=layer, tf=tf, apply_final_norm=apply_final_norm),
        grid=(m // tm,),
        in_specs=[
            pl.BlockSpec((tm, d), lambda i: (i, 0)),
            pl.BlockSpec((1, d), lambda i: (0, 0)),
            pl.BlockSpec(memory_space=pl.ANY),
            pl.BlockSpec(memory_space=pl.ANY),
            pl.BlockSpec((1, d), lambda i: (0, 0)),
        ],
        out_specs=pl.BlockSpec((tm, d), lambda i: (i, 0)),
        out_shape=jax.ShapeDtypeStruct((m, d), F32),
        scratch_shapes=[
            pltpu.VMEM((tm, d), BF16),
            pltpu.VMEM((2, d, tf), F32),
            pltpu.VMEM((2, tf, d), F32),
            pltpu.SemaphoreType.DMA((2, 2)),
        ],
        compiler_params=pltpu.CompilerParams(
            dimension_semantics=("arbitrary",),
            vmem_limit_bytes=VMEM_LIMIT_BYTES),
        name="mlp_final" if apply_final_norm else "mlp",
    )(h, g, w1, w2, final_g)


def _pool_kernel(x_ref, g_ref, win_ref, wgrp_ref, scale_ref, wout_ref, o_ref,
                 vbuf_ref, mixed_ref, *, seq):
    tm, width = mixed_ref.shape
    head = width // len(B_WINDOWS)
    i = pl.program_id(0)
    start = (i * tm) % seq

    @pl.when(start == 0)
    def _():
        vbuf_ref[0:HALO, :] = jnp.zeros((HALO, width), F32)

    @pl.when(start != 0)
    def _():
        vbuf_ref[0:HALO, :] = vbuf_ref[tm:tm + HALO, :]

    x = x_ref[...]
    hn = _rmsnorm(x, g_ref[...]).astype(BF16)
    vbuf_ref[HALO:HALO + tm, :] = jnp.dot(hn, win_ref[...].astype(BF16),
                                          preferred_element_type=F32)

    pos = start + lax.broadcasted_iota(jnp.int32, (tm, 1), 0)
    for g, w in enumerate(B_WINDOWS):
        cols = slice(g * head, (g + 1) * head)
        v = vbuf_ref[HALO:HALO + tm, cols]
        total = v
        for j in range(1, w):
            total = total + vbuf_ref[HALO - j:HALO - j + tm, cols]
        inv_count = 1.0 / jnp.minimum(pos + 1, w).astype(F32)
        pooled = (total * inv_count - v).astype(BF16)
        mixed = jnp.dot(pooled, wgrp_ref[g].astype(BF16), preferred_element_type=F32)
        mixed_ref[:, cols] = (mixed * scale_ref[:, cols]).astype(BF16)

    o_ref[...] = x + jnp.dot(mixed_ref[...], wout_ref[...], preferred_element_type=F32)


def _pool_mixer(x, g, w_in, w_grp, scale, w_out, *, tm, seq):
    m, d = x.shape
    width = w_in.shape[1]
    return pl.pallas_call(
        functools.partial(_pool_kernel, seq=seq),
        grid=(m // tm,),
        in_specs=[
            pl.BlockSpec((tm, d), lambda i: (i, 0)),
            _const_spec((1, d)),
            _const_spec(w_in.shape),
            _const_spec(w_grp.shape),
            _const_spec((1, width)),
            _const_spec(w_out.shape),
        ],
        out_specs=pl.BlockSpec((tm, d), lambda i: (i, 0)),
        out_shape=jax.ShapeDtypeStruct((m, d), F32),
        scratch_shapes=[pltpu.VMEM((HALO + tm, width), F32), pltpu.VMEM((tm, width), BF16)],
        compiler_params=pltpu.CompilerParams(
            dimension_semantics=("arbitrary",),
            vmem_limit_bytes=VMEM_LIMIT_BYTES),
        name="pool_mixer",
    )(x, g, w_in, w_grp, scale, w_out)


def kernel(x, a_w_in, a_ln_g, a_ln_b, a_w_s, a_b_s, a_w_out, b_w_in, b_w_grp, b_scale, b_w_out,
           norm_mix, norm_mlp, mlp_w1, mlp_w2, final_norm):
    bsz, seq, d = x.shape
    depth = norm_mix.shape[0]
    h = x.reshape(bsz * seq, d)
    row = lambda p: p.reshape(1, -1)
    for i in range(depth):
        j = i // 2
        if i % 2 == 0:
            z = _gmlp_in(h, row(norm_mix[i]), a_w_in, j, tm=1024, tn=1024)
            h = _gmlp_out(h, z, row(a_ln_g[j]), row(a_ln_b[j]), a_w_s[j], a_b_s[j].T,
                          a_w_out[j], tm=512)
        else:
            h = _pool_mixer(h, row(norm_mix[i]), b_w_in[j], b_w_grp[j], row(b_scale[j]),
                            b_w_out[j].astype(BF16), tm=512, seq=seq)
        h = _mlp(h, row(norm_mlp[i]), mlp_w1, mlp_w2, row(final_norm), i,
                 tm=1024, tf=512, apply_final_norm=(i == depth - 1))
    return h.reshape(bsz, seq, d)
```

```python
import functools

import numpy as np
import jax
import jax.numpy as jnp
from jax import lax
from jax.experimental import pallas as pl
from jax.experimental.pallas import tpu as pltpu

EPS = 1e-6
CHUNK = 128
A_GROUPS = 8
B_WINDOWS = (2, 4, 8, 16)
HALO = 16

VMEM_LIMIT_BYTES = 60 * 1024 * 1024

F32 = jnp.float32
BF16 = jnp.bfloat16


def _rmsnorm(x, g):
    return x * lax.rsqrt(jnp.mean(x * x, axis=-1, keepdims=True) + EPS) * g


def _const_spec(shape):
    return pl.BlockSpec(shape, lambda *_: (0,) * len(shape), pipeline_mode=pl.Buffered(1))


def _gmlp_in_kernel(x_ref, g_ref, w_hbm, z_ref, hn_ref, w_buf, sem, *, layer, tn):
    i = pl.program_id(0)
    n_tiles = w_hbm.shape[2] // tn

    n_parts = sem.shape[1]
    rows = w_hbm.shape[1] // n_parts

    def copies(t, slot):
        return [
            pltpu.make_async_copy(
                w_hbm.at[layer, p * rows:(p + 1) * rows, t * tn:(t + 1) * tn],
                w_buf.at[slot, p * rows:(p + 1) * rows, :], sem.at[slot, p])
            for p in range(n_parts)]

    def start(t, slot):
        for c in copies(t, slot):
            c.start()

    @pl.when(i == 0)
    def _():
        start(0, 0)

    hn_ref[...] = _rmsnorm(x_ref[...], g_ref[...]).astype(BF16)

    sqrt_half = np.float32(np.sqrt(0.5))
    for t in range(n_tiles):
        slot = t % 2
        if t + 1 < n_tiles:
            start(t + 1, 1 - slot)
        else:
            @pl.when(i != pl.num_programs(0) - 1)
            def _():
                start(0, 1 - slot)
        for c in copies(t, slot):
            c.wait()
        z = jnp.dot(hn_ref[...], w_buf[slot].astype(BF16), preferred_element_type=F32)
        z_ref[:, t * tn:(t + 1) * tn] = (0.5 * z * (1.0 + lax.erf(z * sqrt_half))).astype(BF16)


def _gmlp_in(x, g, w, layer, *, tm, tn):
    m, d = x.shape
    n = w.shape[2]
    assert (n // tn) % 2 == 0
    return pl.pallas_call(
        functools.partial(_gmlp_in_kernel, layer=layer, tn=tn),
        grid=(m // tm,),
        in_specs=[
            pl.BlockSpec((tm, d), lambda i: (i, 0)),
            pl.BlockSpec((1, d), lambda i: (0, 0)),
            pl.BlockSpec(memory_space=pl.ANY),
        ],
        out_specs=pl.BlockSpec((tm, n), lambda i: (i, 0)),
        out_shape=jax.ShapeDtypeStruct((m, n), BF16),
        scratch_shapes=[
            pltpu.VMEM((tm, d), BF16),
            pltpu.VMEM((2, d, tn), F32),
            pltpu.SemaphoreType.DMA((2, 4)),
        ],
        compiler_params=pltpu.CompilerParams(
            dimension_semantics=("arbitrary",),
            vmem_limit_bytes=VMEM_LIMIT_BYTES),
        name="gmlp_in",
    )(x, g, w)


def _gmlp_out_kernel(x_ref, u_ref, v_ref, lng_ref, lnb_ref, ws_ref, bst_ref, wout_ref,
                     o_ref, vn_ref, gated_ref):
    tm, width = vn_ref.shape
    head = width // A_GROUPS

    v = v_ref[...].astype(F32)
    mu = jnp.mean(v, axis=-1, keepdims=True)
    vc = v - mu
    var = jnp.mean(vc * vc, axis=-1, keepdims=True)
    vn_ref[...] = (vc * lax.rsqrt(var + EPS) * lng_ref[...] + lnb_ref[...]).astype(BF16)

    row = lax.broadcasted_iota(jnp.int32, (CHUNK, CHUNK), 0)
    col = lax.broadcasted_iota(jnp.int32, (CHUNK, CHUNK), 1)
    causal = (row >= col).astype(F32)
    for g in range(A_GROUPS):
        cols = slice(g * head, (g + 1) * head)
        w_g = (ws_ref[g] * causal).astype(BF16)
        b_g = bst_ref[:, g:g + 1]
        for c in range(tm // CHUNK):
            rows = slice(c * CHUNK, (c + 1) * CHUNK)
            s = jnp.dot(w_g, vn_ref[rows, cols], preferred_element_type=F32) + b_g
            gated_ref[rows, cols] = (u_ref[rows, cols].astype(F32) * s).astype(BF16)

    o_ref[...] = x_ref[...] + jnp.dot(gated_ref[...], wout_ref[...].astype(BF16),
                                      preferred_element_type=F32)


def _gmlp_out(x, z, ln_g, ln_b, w_s, b_s_t, w_out, *, tm):
    m, d = x.shape
    width = w_out.shape[0]
    return pl.pallas_call(
        _gmlp_out_kernel,
        grid=(m // tm,),
        in_specs=[
            pl.BlockSpec((tm, d), lambda i: (i, 0)),
            pl.BlockSpec((tm, width), lambda i: (i, 0)),
            pl.BlockSpec((tm, width), lambda i: (i, 1)),
            _const_spec((1, width)),
            _const_spec((1, width)),
            _const_spec(w_s.shape),
            _const_spec(b_s_t.shape),
            _const_spec(w_out.shape),
        ],
        out_specs=pl.BlockSpec((tm, d), lambda i: (i, 0)),
        out_shape=jax.ShapeDtypeStruct((m, d), F32),
        scratch_shapes=[pltpu.VMEM((tm, width), BF16), pltpu.VMEM((tm, width), BF16)],
        compiler_params=pltpu.CompilerParams(
            dimension_semantics=("parallel",),
            vmem_limit_bytes=VMEM_LIMIT_BYTES),
        name="gmlp_out",
    )(x, z, z, ln_g, ln_b, w_s, b_s_t, w_out)


def _mlp_kernel(h_ref, g_ref, w1_hbm, w2_hbm, fg_ref, o_ref, hn_ref, w1_buf, w2_buf, sem,
                *, layer, tf, apply_final_norm):
    i = pl.program_id(0)
    n_tiles = w1_hbm.shape[2] // tf
    assert n_tiles % 2 == 0

    def copies(t, slot):
        off = t * tf if isinstance(t, int) else pl.multiple_of(t * tf, tf)
        return (
            pltpu.make_async_copy(w1_hbm.at[layer, :, pl.ds(off, tf)], w1_buf.at[slot],
                                  sem.at[0, slot]),
            pltpu.make_async_copy(w2_hbm.at[layer, pl.ds(off, tf), :], w2_buf.at[slot],
                                  sem.at[1, slot]),
        )

    def start(t, slot):
        for c in copies(t, slot):
            c.start()

    def wait(t, slot):
        for c in copies(t, slot):
            c.wait()

    @pl.when(i == 0)
    def _():
        start(0, 0)

    h = h_ref[...]
    hn_ref[...] = _rmsnorm(h, g_ref[...]).astype(BF16)
    o_ref[...] = h

    def accumulate(slot):
        a = jnp.dot(hn_ref[...], w1_buf[slot].astype(BF16), preferred_element_type=F32)
        a = jnp.maximum(a, 0.0)
        o_ref[...] += jnp.dot((a * a).astype(BF16), w2_buf[slot].astype(BF16),
                              preferred_element_type=F32)

    def pair(p, carry):
        t = 2 * p
        start(t + 1, 1)
        wait(t, 0)
        accumulate(0)

        is_last_tile = t + 2 == n_tiles

        @pl.when(jnp.logical_not(jnp.logical_and(is_last_tile, i == pl.num_programs(0) - 1)))
        def _():
            start(jnp.where(is_last_tile, 0, t + 2), 0)

        wait(t + 1, 1)
        accumulate(1)
        return carry

    lax.fori_loop(0, n_tiles // 2, pair, 0)

    if apply_final_norm:
        o_ref[...] = _rmsnorm(o_ref[...], fg_ref[...])


def _mlp(h, g, w1, w2, final_g, layer, *, tm, tf, apply_final_norm):
    m, d = h.shape
    return pl.pallas_call(
        functools.partial(_mlp_kernel, layer=layer, tf=tf, apply_final_norm=apply_final_norm),
        grid=(m // tm,),
        in_specs=[
            pl.BlockSpec((tm, d), lambda i: (i, 0)),
            pl.BlockSpec((1, d), lambda i: (0, 0)),
            pl.BlockSpec(memory_space=pl.ANY),
            pl.BlockSpec(memory_space=pl.ANY),
            pl.BlockSpec((1, d), lambda i: (0, 0)),
        ],
        out_specs=pl.BlockSpec((tm, d), lambda i: (i, 0)),
        out_shape=jax.ShapeDtypeStruct((m, d), F32),
        scratch_shapes=[
            pltpu.VMEM((tm, d), BF16),
            pltpu.VMEM((2, d, tf), F32),
            pltpu.VMEM((2, tf, d), F32),
            pltpu.SemaphoreType.DMA((2, 2)),
        ],
        compiler_params=pltpu.CompilerParams(
            dimension_semantics=("arbitrary",),
            vmem_limit_bytes=VMEM_LIMIT_BYTES),
        name="mlp_final" if apply_final_norm else "mlp",
    )(h, g, w1, w2, final_g)


def _pool_kernel(x_ref, g_ref, win_ref, wgrp_ref, scale_ref, wout_ref, o_ref,
                 vbuf_ref, mixed_ref, *, seq):
    tm, width = mixed_ref.shape
    head = width // len(B_WINDOWS)
    i = pl.program_id(0)
    start = (i * tm) % seq

    @pl.when(start == 0)
    def _():
        vbuf_ref[0:HALO, :] = jnp.zeros((HALO, width), F32)

    @pl.when(start != 0)
    def _():
        vbuf_ref[0:HALO, :] = vbuf_ref[tm:tm + HALO, :]

    x = x_ref[...]
    hn = _rmsnorm(x, g_ref[...]).astype(BF16)
    vbuf_ref[HALO:HALO + tm, :] = jnp.dot(hn, win_ref[...].astype(BF16),
                                          preferred_element_type=F32)

    pos = start + lax.broadcasted_iota(jnp.int32, (tm, 1), 0)
    for g, w in enumerate(B_WINDOWS):
        cols = slice(g * head, (g + 1) * head)
        v = vbuf_ref[HALO:HALO + tm, cols]
        total = v
        for j in range(1, w):
            total = total + vbuf_ref[HALO - j:HALO - j + tm, cols]
        inv_count = 1.0 / jnp.minimum(pos + 1, w).astype(F32)
        pooled = (total * inv_count - v).astype(BF16)
        mixed = jnp.dot(pooled, wgrp_ref[g].astype(BF16), preferred_element_type=F32)
        mixed_ref[:, cols] = (mixed * scale_ref[:, cols]).astype(BF16)

    o_ref[...] = x + jnp.dot(mixed_ref[...], wout_ref[...].astype(BF16),
                             preferred_element_type=F32)


def _pool_mixer(x, g, w_in, w_grp, scale, w_out, *, tm, seq):
    m, d = x.shape
    width = w_in.shape[1]
    return pl.pallas_call(
        functools.partial(_pool_kernel, seq=seq),
        grid=(m // tm,),
        in_specs=[
            pl.BlockSpec((tm, d), lambda i: (i, 0)),
            _const_spec((1, d)),
            _const_spec(w_in.shape),
            _const_spec(w_grp.shape),
            _const_spec((1, width)),
            _const_spec(w_out.shape),
        ],
        out_specs=pl.BlockSpec((tm, d), lambda i: (i, 0)),
        out_shape=jax.ShapeDtypeStruct((m, d), F32),
        scratch_shapes=[pltpu.VMEM((HALO + tm, width), F32), pltpu.VMEM((tm, width), BF16)],
        compiler_params=pltpu.CompilerParams(
            dimension_semantics=("arbitrary",),
            vmem_limit_bytes=VMEM_LIMIT_BYTES),
        name="pool_mixer",
    )(x, g, w_in, w_grp, scale, w_out)


def kernel(x, a_w_in, a_ln_g, a_ln_b, a_w_s, a_b_s, a_w_out, b_w_in, b_w_grp, b_scale, b_w_out,
           norm_mix, norm_mlp, mlp_w1, mlp_w2, final_norm):
    bsz, seq, d = x.shape
    depth = norm_mix.shape[0]
    h = x.reshape(bsz * seq, d)
    row = lambda p: p.reshape(1, -1)
    for i in range(depth):
        j = i // 2
        if i % 2 == 0:
            z = _gmlp_in(h, row(norm_mix[i]), a_w_in, j, tm=1024, tn=1024)
            h = _gmlp_out(h, z, row(a_ln_g[j]), row(a_ln_b[j]), a_w_s[j], a_b_s[j].T,
                          a_w_out[j], tm=512)
        else:
            h = _pool_mixer(h, row(norm_mix[i]), b_w_in[j], b_w_grp[j], row(b_scale[j]),
                            b_w_out[j], tm=512, seq=seq)
        h = _mlp(h, row(norm_mlp[i]), mlp_w1, mlp_w2, row(final_norm), i,
                 tm=1024, tf=512, apply_final_norm=(i == depth - 1))
    return h.reshape(bsz, seq, d)
```

```python
import functools

import numpy as np
import jax
import jax.numpy as jnp
from jax import lax
from jax.experimental import pallas as pl
from jax.experimental.pallas import tpu as pltpu

EPS = 1e-6
CHUNK = 128
A_GROUPS = 8
B_WINDOWS = (2, 4, 8, 16)
MLP_ROW_BLOCK = 256
HALO = 16

VMEM_LIMIT_BYTES = 60 * 1024 * 1024

F32 = jnp.float32
BF16 = jnp.bfloat16


def _rmsnorm(x, g):
    return x * lax.rsqrt(jnp.mean(x * x, axis=-1, keepdims=True) + EPS) * g


def _const_spec(shape):
    return pl.BlockSpec(shape, lambda *_: (0,) * len(shape), pipeline_mode=pl.Buffered(1))


def _gmlp_in_kernel(x_ref, g_ref, w_hbm, z_ref, hn_ref, w_buf, sem, *, layer, tn):
    i = pl.program_id(0)
    n_tiles = w_hbm.shape[2] // tn

    n_parts = sem.shape[1]
    rows = w_hbm.shape[1] // n_parts

    def copies(t, slot):
        return [
            pltpu.make_async_copy(
                w_hbm.at[layer, p * rows:(p + 1) * rows, t * tn:(t + 1) * tn],
                w_buf.at[slot, p * rows:(p + 1) * rows, :], sem.at[slot, p])
            for p in range(n_parts)]

    def start(t, slot):
        for c in copies(t, slot):
            c.start()

    @pl.when(i == 0)
    def _():
        start(0, 0)

    hn_ref[...] = _rmsnorm(x_ref[...], g_ref[...]).astype(BF16)

    sqrt_half = np.float32(np.sqrt(0.5))
    for t in range(n_tiles):
        slot = t % 2
        if t + 1 < n_tiles:
            start(t + 1, 1 - slot)
        else:
            @pl.when(i != pl.num_programs(0) - 1)
            def _():
                start(0, 1 - slot)
        for c in copies(t, slot):
            c.wait()
        z = jnp.dot(hn_ref[...], w_buf[slot].astype(BF16), preferred_element_type=F32)
        z_ref[:, t * tn:(t + 1) * tn] = (0.5 * z * (1.0 + lax.erf(z * sqrt_half))).astype(BF16)


def _gmlp_in(x, g, w, layer, *, tm, tn):
    m, d = x.shape
    n = w.shape[2]
    assert (n // tn) % 2 == 0
    return pl.pallas_call(
        functools.partial(_gmlp_in_kernel, layer=layer, tn=tn),
        grid=(m // tm,),
        in_specs=[
            pl.BlockSpec((tm, d), lambda i: (i, 0)),
            pl.BlockSpec((1, d), lambda i: (0, 0)),
            pl.BlockSpec(memory_space=pl.ANY),
        ],
        out_specs=pl.BlockSpec((tm, n), lambda i: (i, 0)),
        out_shape=jax.ShapeDtypeStruct((m, n), BF16),
        scratch_shapes=[
            pltpu.VMEM((tm, d), BF16),
            pltpu.VMEM((2, d, tn), F32),
            pltpu.SemaphoreType.DMA((2, 4)),
        ],
        compiler_params=pltpu.CompilerParams(
            dimension_semantics=("arbitrary",),
            vmem_limit_bytes=VMEM_LIMIT_BYTES),
        name="gmlp_in",
    )(x, g, w)


def _gmlp_out_kernel(x_ref, u_ref, v_ref, lng_ref, lnb_ref, ws_ref, bst_ref, wout_ref,
                     o_ref, vn_ref, gated_ref):
    tm, width = vn_ref.shape
    head = width // A_GROUPS

    v = v_ref[...].astype(F32)
    mu = jnp.mean(v, axis=-1, keepdims=True)
    vc = v - mu
    var = jnp.mean(vc * vc, axis=-1, keepdims=True)
    vn_ref[...] = (vc * lax.rsqrt(var + EPS) * lng_ref[...] + lnb_ref[...]).astype(BF16)

    row = lax.broadcasted_iota(jnp.int32, (CHUNK, CHUNK), 0)
    col = lax.broadcasted_iota(jnp.int32, (CHUNK, CHUNK), 1)
    causal = (row >= col).astype(F32)
    for g in range(A_GROUPS):
        cols = slice(g * head, (g + 1) * head)
        w_g = (ws_ref[g] * causal).astype(BF16)
        b_g = bst_ref[:, g:g + 1]
        for c in range(tm // CHUNK):
            rows = slice(c * CHUNK, (c + 1) * CHUNK)
            s = jnp.dot(w_g, vn_ref[rows, cols], preferred_element_type=F32) + b_g
            gated_ref[rows, cols] = (u_ref[rows, cols].astype(F32) * s).astype(BF16)

    o_ref[...] = x_ref[...] + jnp.dot(gated_ref[...], wout_ref[...].astype(BF16),
                                      preferred_element_type=F32)


def _gmlp_out(x, z, ln_g, ln_b, w_s, b_s_t, w_out, *, tm):
    m, d = x.shape
    width = w_out.shape[0]
    return pl.pallas_call(
        _gmlp_out_kernel,
        grid=(m // tm,),
        in_specs=[
            pl.BlockSpec((tm, d), lambda i: (i, 0)),
            pl.BlockSpec((tm, width), lambda i: (i, 0)),
            pl.BlockSpec((tm, width), lambda i: (i, 1)),
            _const_spec((1, width)),
            _const_spec((1, width)),
            _const_spec(w_s.shape),
            _const_spec(b_s_t.shape),
            _const_spec(w_out.shape),
        ],
        out_specs=pl.BlockSpec((tm, d), lambda i: (i, 0)),
        out_shape=jax.ShapeDtypeStruct((m, d), F32),
        scratch_shapes=[pltpu.VMEM((tm, width), BF16), pltpu.VMEM((tm, width), BF16)],
        compiler_params=pltpu.CompilerParams(
            dimension_semantics=("parallel",),
            vmem_limit_bytes=VMEM_LIMIT_BYTES),
        name="gmlp_out",
    )(x, z, z, ln_g, ln_b, w_s, b_s_t, w_out)


def _mlp_kernel(h_ref, g_ref, w1_hbm, w2_hbm, fg_ref, o_ref, hn_ref, w1_buf, w2_buf, sem,
                *, layer, tf, apply_final_norm):
    i = pl.program_id(0)
    n_tiles = w1_hbm.shape[2] // tf
    assert n_tiles % 2 == 0

    def copies(t, slot):
        off = t * tf if isinstance(t, int) else pl.multiple_of(t * tf, tf)
        return (
            pltpu.make_async_copy(w1_hbm.at[layer, :, pl.ds(off, tf)], w1_buf.at[slot],
                                  sem.at[0, slot]),
            pltpu.make_async_copy(w2_hbm.at[layer, pl.ds(off, tf), :], w2_buf.at[slot],
                                  sem.at[1, slot]),
        )

    def start(t, slot):
        for c in copies(t, slot):
            c.start()

    def wait(t, slot):
        for c in copies(t, slot):
            c.wait()

    tm = h_ref.shape[0]
    row_blocks = [slice(r, r + MLP_ROW_BLOCK) for r in range(0, tm, MLP_ROW_BLOCK)]

    def hidden(hn, slot):
        a = jnp.dot(hn, w1_buf[slot].astype(BF16), preferred_element_type=F32)
        a = jnp.maximum(a, 0.0)
        return jnp.dot((a * a).astype(BF16), w2_buf[slot].astype(BF16),
                       preferred_element_type=F32)

    def accumulate(slot):
        o_ref[...] += hidden(hn_ref[...], slot)

    @pl.when(i == 0)
    def _():
        start(0, 0)

    start(1, 1)
    wait(0, 0)
    for rows in row_blocks:
        h = h_ref[rows, :]
        hn = _rmsnorm(h, g_ref[...]).astype(BF16)
        hn_ref[rows, :] = hn
        o_ref[rows, :] = h + hidden(hn, 0)

    def pair(p, carry):
        t = 2 * p + 1
        start(t + 1, 0)
        wait(t, 1)
        accumulate(1)
        start(t + 2, 1)
        wait(t + 1, 0)
        accumulate(0)
        return carry

    lax.fori_loop(0, (n_tiles - 2) // 2, pair, 0)

    @pl.when(i != pl.num_programs(0) - 1)
    def _():
        start(0, 0)

    wait(n_tiles - 1, 1)
    if apply_final_norm:
        for rows in row_blocks:
            y = o_ref[rows, :] + hidden(hn_ref[rows, :], 1)
            o_ref[rows, :] = _rmsnorm(y, fg_ref[...])
    else:
        accumulate(1)


def _mlp(h, g, w1, w2, final_g, layer, *, tm, tf, apply_final_norm):
    m, d = h.shape
    return pl.pallas_call(
        functools.partial(_mlp_kernel, layer=layer, tf=tf, apply_final_norm=apply_final_norm),
        grid=(m // tm,),
        in_specs=[
            pl.BlockSpec((tm, d), lambda i: (i, 0)),
            pl.BlockSpec((1, d), lambda i: (0, 0)),
            pl.BlockSpec(memory_space=pl.ANY),
            pl.BlockSpec(memory_space=pl.ANY),
            pl.BlockSpec((1, d), lambda i: (0, 0)),
        ],
        out_specs=pl.BlockSpec((tm, d), lambda i: (i, 0)),
        out_shape=jax.ShapeDtypeStruct((m, d), F32),
        scratch_shapes=[
            pltpu.VMEM((tm, d), BF16),
            pltpu.VMEM((2, d, tf), F32),
            pltpu.VMEM((2, tf, d), F32),
            pltpu.SemaphoreType.DMA((2, 2)),
        ],
        compiler_params=pltpu.CompilerParams(
            dimension_semantics=("arbitrary",),
            vmem_limit_bytes=VMEM_LIMIT_BYTES),
        name="mlp_final" if apply_final_norm else "mlp",
    )(h, g, w1, w2, final_g)


def _pool_kernel(x_ref, g_ref, win_ref, wgrp_ref, scale_ref, wout_ref, o_ref,
                 vbuf_ref, mixed_ref, *, seq):
    tm, width = mixed_ref.shape
    head = width // len(B_WINDOWS)
    i = pl.program_id(0)
    start = (i * tm) % seq

    @pl.when(start == 0)
    def _():
        vbuf_ref[0:HALO, :] = jnp.zeros((HALO, width), F32)

    @pl.when(start != 0)
    def _():
        vbuf_ref[0:HALO, :] = vbuf_ref[tm:tm + HALO, :]

    x = x_ref[...]
    hn = _rmsnorm(x, g_ref[...]).astype(BF16)
    vbuf_ref[HALO:HALO + tm, :] = jnp.dot(hn, win_ref[...].astype(BF16),
                                          preferred_element_type=F32)

    pos = start + lax.broadcasted_iota(jnp.int32, (tm, 1), 0)
    for g, w in enumerate(B_WINDOWS):
        cols = slice(g * head, (g + 1) * head)
        v = vbuf_ref[HALO:HALO + tm, cols]
        total = v
        for j in range(1, w):
            total = total + vbuf_ref[HALO - j:HALO - j + tm, cols]
        inv_count = 1.0 / jnp.minimum(pos + 1, w).astype(F32)
        pooled = (total * inv_count - v).astype(BF16)
        mixed = jnp.dot(pooled, wgrp_ref[g].astype(BF16), preferred_element_type=F32)
        mixed_ref[:, cols] = (mixed * scale_ref[:, cols]).astype(BF16)

    o_ref[...] = x + jnp.dot(mixed_ref[...], wout_ref[...].astype(BF16),
                             preferred_element_type=F32)


def _pool_mixer(x, g, w_in, w_grp, scale, w_out, *, tm, seq):
    m, d = x.shape
    width = w_in.shape[1]
    return pl.pallas_call(
        functools.partial(_pool_kernel, seq=seq),
        grid=(m // tm,),
        in_specs=[
            pl.BlockSpec((tm, d), lambda i: (i, 0)),
            _const_spec((1, d)),
            _const_spec(w_in.shape),
            _const_spec(w_grp.shape),
            _const_spec((1, width)),
            _const_spec(w_out.shape),
        ],
        out_specs=pl.BlockSpec((tm, d), lambda i: (i, 0)),
        out_shape=jax.ShapeDtypeStruct((m, d), F32),
        scratch_shapes=[pltpu.VMEM((HALO + tm, width), F32), pltpu.VMEM((tm, width), BF16)],
        compiler_params=pltpu.CompilerParams(
            dimension_semantics=("arbitrary",),
            vmem_limit_bytes=VMEM_LIMIT_BYTES),
        name="pool_mixer",
    )(x, g, w_in, w_grp, scale, w_out)


def kernel(x, a_w_in, a_ln_g, a_ln_b, a_w_s, a_b_s, a_w_out, b_w_in, b_w_grp, b_scale, b_w_out,
           norm_mix, norm_mlp, mlp_w1, mlp_w2, final_norm):
    bsz, seq, d = x.shape
    depth = norm_mix.shape[0]
    h = x.reshape(bsz * seq, d)
    row = lambda p: p.reshape(1, -1)
    for i in range(depth):
        j = i // 2
        if i % 2 == 0:
            z = _gmlp_in(h, row(norm_mix[i]), a_w_in, j, tm=1024, tn=1024)
            h = _gmlp_out(h, z, row(a_ln_g[j]), row(a_ln_b[j]), a_w_s[j], a_b_s[j].T,
                          a_w_out[j], tm=512)
        else:
            h = _pool_mixer(h, row(norm_mix[i]), b_w_in[j], b_w_grp[j], row(b_scale[j]),
                            b_w_out[j], tm=512, seq=seq)
        h = _mlp(h, row(norm_mlp[i]), mlp_w1, mlp_w2, row(final_norm), i,
                 tm=1024, tf=512, apply_final_norm=(i == depth - 1))
    return h.reshape(bsz, seq, d)
```

```python
import functools

import numpy as np
import jax
import jax.numpy as jnp
from jax import lax
from jax.experimental import pallas as pl
from jax.experimental.pallas import tpu as pltpu

EPS = 1e-6
CHUNK = 128
A_GROUPS = 8
B_WINDOWS = (2, 4, 8, 16)
HALO = 16

VMEM_LIMIT_BYTES = 60 * 1024 * 1024

F32 = jnp.float32
BF16 = jnp.bfloat16


def _rmsnorm(x, g):
    return x * lax.rsqrt(jnp.mean(x * x, axis=-1, keepdims=True) + EPS) * g


def _const_spec(shape):
    return pl.BlockSpec(shape, lambda *_: (0,) * len(shape), pipeline_mode=pl.Buffered(1))


def _gmlp_in_kernel(x_ref, g_ref, w_ref, z_ref, hn_ref):
    @pl.when(pl.program_id(1) == 0)
    def _():
        hn_ref[...] = _rmsnorm(x_ref[...], g_ref[...]).astype(BF16)

    z = jnp.dot(hn_ref[...], w_ref[...].astype(BF16), preferred_element_type=F32)
    sqrt_half = np.float32(np.sqrt(0.5))
    z_ref[...] = (0.5 * z * (1.0 + lax.erf(z * sqrt_half))).astype(BF16)


def _gmlp_in(x, g, w, layer, *, tm, tn):
    m, d = x.shape
    n = w.shape[2]
    return pl.pallas_call(
        _gmlp_in_kernel,
        grid=(m // tm, n // tn),
        in_specs=[
            pl.BlockSpec((tm, d), lambda i, j: (i, 0)),
            pl.BlockSpec((1, d), lambda i, j: (0, 0)),
            pl.BlockSpec((None, d, tn), lambda i, j: (layer, 0, j)),
        ],
        out_specs=pl.BlockSpec((tm, tn), lambda i, j: (i, j)),
        out_shape=jax.ShapeDtypeStruct((m, n), BF16),
        scratch_shapes=[pltpu.VMEM((tm, d), BF16)],
        compiler_params=pltpu.CompilerParams(
            dimension_semantics=("parallel", "arbitrary"),
            vmem_limit_bytes=VMEM_LIMIT_BYTES),
        name="gmlp_in",
    )(x, g, w)


def _gmlp_out_kernel(x_ref, u_ref, v_ref, lng_ref, lnb_ref, ws_ref, bst_ref, wout_ref,
                     o_ref, vn_ref, gated_ref):
    tm, width = vn_ref.shape
    head = width // A_GROUPS

    v = v_ref[...].astype(F32)
    mu = jnp.mean(v, axis=-1, keepdims=True)
    vc = v - mu
    var = jnp.mean(vc * vc, axis=-1, keepdims=True)
    vn_ref[...] = (vc * lax.rsqrt(var + EPS) * lng_ref[...] + lnb_ref[...]).astype(BF16)

    row = lax.broadcasted_iota(jnp.int32, (CHUNK, CHUNK), 0)
    col = lax.broadcasted_iota(jnp.int32, (CHUNK, CHUNK), 1)
    causal = (row >= col).astype(F32)
    for g in range(A_GROUPS):
        cols = slice(g * head, (g + 1) * head)
        w_g = (ws_ref[g] * causal).astype(BF16)
        b_g = bst_ref[:, g:g + 1]
        for c in range(tm // CHUNK):
            rows = slice(c * CHUNK, (c + 1) * CHUNK)
            s = jnp.dot(w_g, vn_ref[rows, cols], preferred_element_type=F32) + b_g
            gated_ref[rows, cols] = (u_ref[rows, cols].astype(F32) * s).astype(BF16)

    o_ref[...] = x_ref[...] + jnp.dot(gated_ref[...], wout_ref[...].astype(BF16),
                                      preferred_element_type=F32)


def _gmlp_out(x, z, ln_g, ln_b, w_s, b_s_t, w_out, *, tm):
    m, d = x.shape
    width = w_out.shape[0]
    return pl.pallas_call(
        _gmlp_out_kernel,
        grid=(m // tm,),
        in_specs=[
            pl.BlockSpec((tm, d), lambda i: (i, 0)),
            pl.BlockSpec((tm, width), lambda i: (i, 0)),
            pl.BlockSpec((tm, width), lambda i: (i, 1)),
            _const_spec((1, width)),
            _const_spec((1, width)),
            _const_spec(w_s.shape),
            _const_spec(b_s_t.shape),
            _const_spec(w_out.shape),
        ],
        out_specs=pl.BlockSpec((tm, d), lambda i: (i, 0)),
        out_shape=jax.ShapeDtypeStruct((m, d), F32),
        scratch_shapes=[pltpu.VMEM((tm, width), BF16), pltpu.VMEM((tm, width), BF16)],
        compiler_params=pltpu.CompilerParams(
            dimension_semantics=("parallel",),
            vmem_limit_bytes=VMEM_LIMIT_BYTES),
        name="gmlp_out",
    )(x, z, z, ln_g, ln_b, w_s, b_s_t, w_out)


def _mlp_kernel(h_ref, g_ref, w1_hbm, w2_hbm, fg_ref, o_ref, hn_ref, w1_buf, w2_buf, sem,
                *, layer, tf, apply_final_norm):
    i = pl.program_id(0)
    n_tiles = w1_hbm.shape[2] // tf
    assert n_tiles % 2 == 0

    def copies(t, slot):
        off = t * tf if isinstance(t, int) else pl.multiple_of(t * tf, tf)
        return (
            pltpu.make_async_copy(w1_hbm.at[layer, :, pl.ds(off, tf)], w1_buf.at[slot],
                                  sem.at[0, slot]),
            pltpu.make_async_copy(w2_hbm.at[layer, pl.ds(off, tf), :], w2_buf.at[slot],
                                  sem.at[1, slot]),
        )

    def start(t, slot):
        for c in copies(t, slot):
            c.start()

    def wait(t, slot):
        for c in copies(t, slot):
            c.wait()

    @pl.when(i == 0)
    def _():
        start(0, 0)

    h = h_ref[...]
    hn_ref[...] = _rmsnorm(h, g_ref[...]).astype(BF16)
    o_ref[...] = h

    def accumulate(slot):
        a = jnp.dot(hn_ref[...], w1_buf[slot].astype(BF16), preferred_element_type=F32)
        a = jnp.maximum(a, 0.0)
        o_ref[...] += jnp.dot((a * a).astype(BF16), w2_buf[slot].astype(BF16),
                              preferred_element_type=F32)

    def pair(p, carry):
        t = 2 * p
        start(t + 1, 1)
        wait(t, 0)
        accumulate(0)

        is_last_tile = t + 2 == n_tiles

        @pl.when(jnp.logical_not(jnp.logical_and(is_last_tile, i == pl.num_programs(0) - 1)))
        def _():
            start(jnp.where(is_last_tile, 0, t + 2), 0)

        wait(t + 1, 1)
        accumulate(1)
        return carry

    lax.fori_loop(0, n_tiles // 2, pair, 0)

    if apply_final_norm:
        o_ref[...] = _rmsnorm(o_ref[...], fg_ref[...])


def _mlp(h, g, w1, w2, final_g, layer, *, tm, tf, apply_final_norm):
    m, d = h.shape
    return pl.pallas_call(
        functools.partial(_mlp_kernel, layer=layer, tf=tf, apply_final_norm=apply_final_norm),
        grid=(m // tm,),
        in_specs=[
            pl.BlockSpec((tm, d), lambda i: (i, 0)),
            pl.BlockSpec((1, d), lambda i: (0, 0)),
            pl.BlockSpec(memory_space=pl.ANY),
            pl.BlockSpec(memory_space=pl.ANY),
            pl.BlockSpec((1, d), lambda i: (0, 0)),
        ],
        out_specs=pl.BlockSpec((tm, d), lambda i: (i, 0)),
        out_shape=jax.ShapeDtypeStruct((m, d), F32),
        scratch_shapes=[
            pltpu.VMEM((tm, d), BF16),
            pltpu.VMEM((2, d, tf), F32),
            pltpu.VMEM((2, tf, d), F32),
            pltpu.SemaphoreType.DMA((2, 2)),
        ],
        compiler_params=pltpu.CompilerParams(
            dimension_semantics=("arbitrary",),
            vmem_limit_bytes=VMEM_LIMIT_BYTES),
        name="mlp_final" if apply_final_norm else "mlp",
    )(h, g, w1, w2, final_g)


def _pool_kernel(x_ref, g_ref, win_ref, wgrp_ref, scale_ref, wout_ref, o_ref,
                 vbuf_ref, mixed_ref, *, seq):
    tm, width = mixed_ref.shape
    head = width // len(B_WINDOWS)
    i = pl.program_id(0)
    start = (i * tm) % seq

    @pl.when(start == 0)
    def _():
        vbuf_ref[0:HALO, :] = jnp.zeros((HALO, width), F32)

    @pl.when(start != 0)
    def _():
        vbuf_ref[0:HALO, :] = vbuf_ref[tm:tm + HALO, :]

    x = x_ref[...]
    hn = _rmsnorm(x, g_ref[...]).astype(BF16)
    vbuf_ref[HALO:HALO + tm, :] = jnp.dot(hn, win_ref[...].astype(BF16),
                                          preferred_element_type=F32)

    pos = start + lax.broadcasted_iota(jnp.int32, (tm, 1), 0)
    for g, w in enumerate(B_WINDOWS):
        cols = slice(g * head, (g + 1) * head)
        v = vbuf_ref[HALO:HALO + tm, cols]
        total = v
        for j in range(1, w):
            total = total + vbuf_ref[HALO - j:HALO - j + tm, cols]
        inv_count = 1.0 / jnp.minimum(pos + 1, w).astype(F32)
        pooled = (total * inv_count - v).astype(BF16)
        mixed = jnp.dot(pooled, wgrp_ref[g].astype(BF16), preferred_element_type=F32)
        mixed_ref[:, cols] = (mixed * scale_ref[:, cols]).astype(BF16)

    o_ref[...] = x + jnp.dot(mixed_ref[...], wout_ref[...].astype(BF16),
                             preferred_element_type=F32)


def _pool_mixer(x, g, w_in, w_grp, scale, w_out, *, tm, seq):
    m, d = x.shape
    width = w_in.shape[1]
    return pl.pallas_call(
        functools.partial(_pool_kernel, seq=seq),
        grid=(m // tm,),
        in_specs=[
            pl.BlockSpec((tm, d), lambda i: (i, 0)),
            _const_spec((1, d)),
            _const_spec(w_in.shape),
            _const_spec(w_grp.shape),
            _const_spec((1, width)),
            _const_spec(w_out.shape),
        ],
        out_specs=pl.BlockSpec((tm, d), lambda i: (i, 0)),
        out_shape=jax.ShapeDtypeStruct((m, d), F32),
        scratch_shapes=[pltpu.VMEM((HALO + tm, width), F32), pltpu.VMEM((tm, width), BF16)],
        compiler_params=pltpu.CompilerParams(
            dimension_semantics=("arbitrary",),
            vmem_limit_bytes=VMEM_LIMIT_BYTES),
        name="pool_mixer",
    )(x, g, w_in, w_grp, scale, w_out)


def kernel(x, a_w_in, a_ln_g, a_ln_b, a_w_s, a_b_s, a_w_out, b_w_in, b_w_grp, b_scale, b_w_out,
           norm_mix, norm_mlp, mlp_w1, mlp_w2, final_norm):
    bsz, seq, d = x.shape
    depth = norm_mix.shape[0]
    h = x.reshape(bsz * seq, d)
    row = lambda p: p.reshape(1, -1)
    for i in range(depth):
        j = i // 2
        if i % 2 == 0:
            z = _gmlp_in(h, row(norm_mix[i]), a_w_in, j, tm=2048, tn=512)
            h = _gmlp_out(h, z, row(a_ln_g[j]), row(a_ln_b[j]), a_w_s[j], a_b_s[j].T,
                          a_w_out[j], tm=512)
        else:
            h = _pool_mixer(h, row(norm_mix[i]), b_w_in[j], b_w_grp[j], row(b_scale[j]),
                            b_w_out[j], tm=512, seq=seq)
        h = _mlp(h, row(norm_mlp[i]), mlp_w1, mlp_w2, row(final_norm), i,
                 tm=1024, tf=512, apply_final_norm=(i == depth - 1))
    return h.reshape(bsz, seq, d)
```

```python
import functools

import numpy as np
import jax
import jax.numpy as jnp
from jax import lax
from jax.experimental import pallas as pl
from jax.experimental.pallas import tpu as pltpu

EPS = 1e-6
CHUNK = 128
A_GROUPS = 8
B_WINDOWS = (2, 4, 8, 16)
SUBLANES = 8
BF16_SUBLANES = 16
HALO = 16

VMEM_LIMIT_BYTES = 60 * 1024 * 1024

F32 = jnp.float32
BF16 = jnp.bfloat16


def _rmsnorm(x, g):
    return x * lax.rsqrt(jnp.mean(x * x, axis=-1, keepdims=True) + EPS) * g


def _const_spec(shape):
    return pl.BlockSpec(shape, lambda *_: (0,) * len(shape), pipeline_mode=pl.Buffered(1))


def _cast_ahead_specs(weights, n_steps, step_of):
    in_specs, out_specs, out_shapes = [], [], []
    for w, layer in weights:
        _, rows, cols = w.shape
        assert rows % (n_steps * BF16_SUBLANES) == 0
        rb = rows // n_steps
        in_specs.append(pl.BlockSpec((None, rb, cols),
                                     lambda *idx, layer=layer: (layer, step_of(*idx), 0)))
        out_specs.append(pl.BlockSpec((rb, cols), lambda *idx: (step_of(*idx), 0)))
        out_shapes.append(jax.ShapeDtypeStruct((rows, cols), BF16))
    return in_specs, out_specs, out_shapes


def _cast_ahead(src_refs, dst_refs):
    for src, dst in zip(src_refs, dst_refs):
        dst[...] = src[...].astype(BF16)


def _gmlp_in_kernel(*refs, n_cast):
    x_ref, g_ref, w_ref = refs[:3]
    cast_src = refs[3:3 + n_cast]
    z_ref = refs[3 + n_cast]
    cast_dst = refs[4 + n_cast:4 + 2 * n_cast]
    hn_ref = refs[4 + 2 * n_cast]

    @pl.when(pl.program_id(1) == 0)
    def _():
        hn_ref[...] = _rmsnorm(x_ref[...], g_ref[...]).astype(BF16)

    z = jnp.dot(hn_ref[...], w_ref[...].astype(BF16), preferred_element_type=F32)
    sqrt_half = np.float32(np.sqrt(0.5))
    z_ref[...] = (0.5 * z * (1.0 + lax.erf(z * sqrt_half))).astype(BF16)
    _cast_ahead(cast_src, cast_dst)


def _gmlp_in(x, g, w, layer, cast, *, tm, tn):
    m, d = x.shape
    n = w.shape[2]
    grid = (m // tm, n // tn)
    c_in, c_out, c_shapes = _cast_ahead_specs(cast, grid[0] * grid[1],
                                              lambda i, j: i * grid[1] + j)
    return pl.pallas_call(
        functools.partial(_gmlp_in_kernel, n_cast=len(cast)),
        grid=grid,
        in_specs=[
            pl.BlockSpec((tm, d), lambda i, j: (i, 0)),
            pl.BlockSpec((1, d), lambda i, j: (0, 0)),
            pl.BlockSpec((None, d, tn), lambda i, j: (layer, 0, j)),
        ] + c_in,
        out_specs=[pl.BlockSpec((tm, tn), lambda i, j: (i, j))] + c_out,
        out_shape=[jax.ShapeDtypeStruct((m, n), BF16)] + c_shapes,
        scratch_shapes=[pltpu.VMEM((tm, d), BF16)],
        compiler_params=pltpu.CompilerParams(
            dimension_semantics=("parallel", "arbitrary"),
            vmem_limit_bytes=VMEM_LIMIT_BYTES),
        name="gmlp_in",
    )(x, g, w, *[cw for cw, _ in cast])


def _gmlp_out_kernel(*refs, n_cast):
    x_ref, u_ref, v_ref, lng_ref, lnb_ref, ws_ref, bst_ref, wout_ref = refs[:8]
    cast_src = refs[8:8 + n_cast]
    o_ref = refs[8 + n_cast]
    cast_dst = refs[9 + n_cast:9 + 2 * n_cast]
    vn_ref, gated_ref = refs[9 + 2 * n_cast:]
    tm, width = vn_ref.shape
    head = width // A_GROUPS

    v = v_ref[...].astype(F32)
    mu = jnp.mean(v, axis=-1, keepdims=True)
    vc = v - mu
    var = jnp.mean(vc * vc, axis=-1, keepdims=True)
    vn_ref[...] = (vc * lax.rsqrt(var + EPS) * lng_ref[...] + lnb_ref[...]).astype(BF16)

    row = lax.broadcasted_iota(jnp.int32, (CHUNK, CHUNK), 0)
    col = lax.broadcasted_iota(jnp.int32, (CHUNK, CHUNK), 1)
    causal = (row >= col).astype(F32)
    for g in range(A_GROUPS):
        cols = slice(g * head, (g + 1) * head)
        w_g = (ws_ref[g] * causal).astype(BF16)
        b_g = bst_ref[:, g:g + 1]
        for c in range(tm // CHUNK):
            rows = slice(c * CHUNK, (c + 1) * CHUNK)
            s = jnp.dot(w_g, vn_ref[rows, cols], preferred_element_type=F32) + b_g
            gated_ref[rows, cols] = (u_ref[rows, cols].astype(F32) * s).astype(BF16)

    o_ref[...] = x_ref[...] + jnp.dot(gated_ref[...], wout_ref[...].astype(BF16),
                                      preferred_element_type=F32)
    _cast_ahead(cast_src, cast_dst)


def _gmlp_out(x, z, ln_g, ln_b, w_s, b_s_t, w_out, cast, *, tm):
    m, d = x.shape
    width = w_out.shape[0]
    c_in, c_out, c_shapes = _cast_ahead_specs(cast, m // tm, lambda i: i)
    return pl.pallas_call(
        functools.partial(_gmlp_out_kernel, n_cast=len(cast)),
        grid=(m // tm,),
        in_specs=[
            pl.BlockSpec((tm, d), lambda i: (i, 0)),
            pl.BlockSpec((tm, width), lambda i: (i, 0)),
            pl.BlockSpec((tm, width), lambda i: (i, 1)),
            _const_spec((1, width)),
            _const_spec((1, width)),
            _const_spec(w_s.shape),
            _const_spec(b_s_t.shape),
            _const_spec(w_out.shape),
        ] + c_in,
        out_specs=[pl.BlockSpec((tm, d), lambda i: (i, 0))] + c_out,
        out_shape=[jax.ShapeDtypeStruct((m, d), F32)] + c_shapes,
        scratch_shapes=[pltpu.VMEM((tm, width), BF16), pltpu.VMEM((tm, width), BF16)],
        compiler_params=pltpu.CompilerParams(
            dimension_semantics=("parallel",),
            vmem_limit_bytes=VMEM_LIMIT_BYTES),
        name="gmlp_out",
    )(x, z, z, ln_g, ln_b, w_s, b_s_t, w_out, *[cw for cw, _ in cast])


def _mlp_kernel(h_ref, g_ref, w1_hbm, w2_hbm, fg_ref, o_ref, hn_ref, w1_buf, w2_buf, sem,
                *, tf, apply_final_norm):
    i = pl.program_id(0)
    n_tiles = w1_hbm.shape[1] // tf
    assert n_tiles % 2 == 0

    def copies(t, slot):
        off = t * tf if isinstance(t, int) else pl.multiple_of(t * tf, tf)
        return (
            pltpu.make_async_copy(w1_hbm.at[:, pl.ds(off, tf)], w1_buf.at[slot],
                                  sem.at[0, slot]),
            pltpu.make_async_copy(w2_hbm.at[pl.ds(off, tf), :], w2_buf.at[slot],
                                  sem.at[1, slot]),
        )

    def start(t, slot):
        for c in copies(t, slot):
            c.start()

    def wait(t, slot):
        for c in copies(t, slot):
            c.wait()

    @pl.when(i == 0)
    def _():
        start(0, 0)

    h = h_ref[...]
    hn_ref[...] = _rmsnorm(h, g_ref[...]).astype(BF16)
    o_ref[...] = h

    def accumulate(slot):
        a = jnp.dot(hn_ref[...], w1_buf[slot], preferred_element_type=F32)
        a = jnp.maximum(a, 0.0)
        o_ref[...] += jnp.dot((a * a).astype(BF16), w2_buf[slot], preferred_element_type=F32)

    def pair(p, carry):
        t = 2 * p
        start(t + 1, 1)
        wait(t, 0)
        accumulate(0)

        is_last_tile = t + 2 == n_tiles

        @pl.when(jnp.logical_not(jnp.logical_and(is_last_tile, i == pl.num_programs(0) - 1)))
        def _():
            start(jnp.where(is_last_tile, 0, t + 2), 0)

        wait(t + 1, 1)
        accumulate(1)
        return carry

    lax.fori_loop(0, n_tiles // 2, pair, 0)

    if apply_final_norm:
        o_ref[...] = _rmsnorm(o_ref[...], fg_ref[...])


def _mlp(h, g, w1, w2, final_g, *, tm, tf, apply_final_norm):
    m, d = h.shape
    assert w1.dtype == BF16 and w2.dtype == BF16
    return pl.pallas_call(
        functools.partial(_mlp_kernel, tf=tf, apply_final_norm=apply_final_norm),
        grid=(m // tm,),
        in_specs=[
            pl.BlockSpec((tm, d), lambda i: (i, 0)),
            pl.BlockSpec((1, d), lambda i: (0, 0)),
            pl.BlockSpec(memory_space=pl.ANY),
            pl.BlockSpec(memory_space=pl.ANY),
            pl.BlockSpec((1, d), lambda i: (0, 0)),
        ],
        out_specs=pl.BlockSpec((tm, d), lambda i: (i, 0)),
        out_shape=jax.ShapeDtypeStruct((m, d), F32),
        scratch_shapes=[
            pltpu.VMEM((tm, d), BF16),
            pltpu.VMEM((2, d, tf), BF16),
            pltpu.VMEM((2, tf, d), BF16),
            pltpu.SemaphoreType.DMA((2, 2)),
        ],
        compiler_params=pltpu.CompilerParams(
            dimension_semantics=("arbitrary",),
            vmem_limit_bytes=VMEM_LIMIT_BYTES),
        name="mlp_final" if apply_final_norm else "mlp",
    )(h, g, w1, w2, final_g)


def _pool_kernel(x_ref, g_ref, win_ref, wgrp_ref, scale_ref, wout_ref, o_ref,
                 vbuf_ref, mixed_ref, *, seq):
    tm, width = mixed_ref.shape
    head = width // len(B_WINDOWS)
    i = pl.program_id(0)
    start = (i * tm) % seq

    @pl.when(start == 0)
    def _():
        vbuf_ref[0:HALO, :] = jnp.zeros((HALO, width), F32)

    @pl.when(start != 0)
    def _():
        vbuf_ref[0:HALO, :] = vbuf_ref[tm:tm + HALO, :]

    x = x_ref[...]
    hn = _rmsnorm(x, g_ref[...]).astype(BF16)
    vbuf_ref[HALO:HALO + tm, :] = jnp.dot(hn, win_ref[...].astype(BF16),
                                          preferred_element_type=F32)

    pos = start + lax.broadcasted_iota(jnp.int32, (tm, 1), 0)
    for g, w in enumerate(B_WINDOWS):
        cols = slice(g * head, (g + 1) * head)
        v = vbuf_ref[HALO:HALO + tm, cols]
        part = vbuf_ref[:, cols]
        span = 1
        while span < min(w, SUBLANES):
            part = part + pltpu.roll(part, span, axis=0)
            span *= 2
        total = part[HALO:, :]
        if w > SUBLANES:
            assert w == 2 * SUBLANES
            total = total + part[HALO - SUBLANES:HALO - SUBLANES + tm, :]
        inv_count = 1.0 / jnp.minimum(pos + 1, w).astype(F32)
        pooled = (total * inv_count - v).astype(BF16)
        mixed = jnp.dot(pooled, wgrp_ref[g].astype(BF16), preferred_element_type=F32)
        mixed_ref[:, cols] = (mixed * scale_ref[:, cols]).astype(BF16)

    o_ref[...] = x + jnp.dot(mixed_ref[...], wout_ref[...].astype(BF16),
                             preferred_element_type=F32)


def _pool_mixer(x, g, w_in, w_grp, scale, w_out, *, tm, seq):
    m, d = x.shape
    width = w_in.shape[1]
    return pl.pallas_call(
        functools.partial(_pool_kernel, seq=seq),
        grid=(m // tm,),
        in_specs=[
            pl.BlockSpec((tm, d), lambda i: (i, 0)),
            _const_spec((1, d)),
            _const_spec(w_in.shape),
            _const_spec(w_grp.shape),
            _const_spec((1, width)),
            _const_spec(w_out.shape),
        ],
        out_specs=pl.BlockSpec((tm, d), lambda i: (i, 0)),
        out_shape=jax.ShapeDtypeStruct((m, d), F32),
        scratch_shapes=[pltpu.VMEM((HALO + tm, width), F32), pltpu.VMEM((tm, width), BF16)],
        compiler_params=pltpu.CompilerParams(
            dimension_semantics=("arbitrary",),
            vmem_limit_bytes=VMEM_LIMIT_BYTES),
        name="pool_mixer",
    )(x, g, w_in, w_grp, scale, w_out)


def kernel(x, a_w_in, a_ln_g, a_ln_b, a_w_s, a_b_s, a_w_out, b_w_in, b_w_grp, b_scale, b_w_out,
           norm_mix, norm_mlp, mlp_w1, mlp_w2, final_norm):
    bsz, seq, d = x.shape
    depth = norm_mix.shape[0]
    assert depth == 2
    h = x.reshape(bsz * seq, d)
    row = lambda p: p.reshape(1, -1)

    z, w1_0, w2_0, w2_1 = _gmlp_in(h, row(norm_mix[0]), a_w_in, 0,
                                   [(mlp_w1, 0), (mlp_w2, 0), (mlp_w2, 1)], tm=1024, tn=512)
    h, w1_1 = _gmlp_out(h, z, row(a_ln_g[0]), row(a_ln_b[0]), a_w_s[0], a_b_s[0].T, a_w_out[0],
                        [(mlp_w1, 1)], tm=512)
    h = _mlp(h, row(norm_mlp[0]), w1_0, w2_0, row(final_norm), tm=1024, tf=1024,
             apply_final_norm=False)

    h = _pool_mixer(h, row(norm_mix[1]), b_w_in[0], b_w_grp[0], row(b_scale[0]), b_w_out[0],
                    tm=512, seq=seq)
    h = _mlp(h, row(norm_mlp[1]), w1_1, w2_1, row(final_norm), tm=1024, tf=1024,
             apply_final_norm=True)
    return h.reshape(bsz, seq, d)
```

```python
import functools

import numpy as np
import jax
import jax.numpy as jnp
from jax import lax
from jax.experimental import pallas as pl
from jax.experimental.pallas import tpu as pltpu

EPS = 1e-6
CHUNK = 128
A_GROUPS = 8
B_WINDOWS = (2, 4, 8, 16)
SUBLANES = 8
BF16_SUBLANES = 16
HALO = 16

VMEM_LIMIT_BYTES = 60 * 1024 * 1024

F32 = jnp.float32
BF16 = jnp.bfloat16


def _rmsnorm(x, g):
    return x * lax.rsqrt(jnp.mean(x * x, axis=-1, keepdims=True) + EPS) * g


def _const_spec(shape):
    return pl.BlockSpec(shape, lambda *_: (0,) * len(shape), pipeline_mode=pl.Buffered(1))


def _cast_ahead_specs(weights, n_steps, step_of):
    in_specs, out_specs, out_shapes = [], [], []
    for w, layer in weights:
        _, rows, cols = w.shape
        assert rows % (n_steps * BF16_SUBLANES) == 0
        rb = rows // n_steps
        in_specs.append(pl.BlockSpec((None, rb, cols),
                                     lambda *idx, layer=layer: (layer, step_of(*idx), 0)))
        out_specs.append(pl.BlockSpec((rb, cols), lambda *idx: (step_of(*idx), 0)))
        out_shapes.append(jax.ShapeDtypeStruct((rows, cols), BF16))
    return in_specs, out_specs, out_shapes


def _cast_ahead(src_refs, dst_refs):
    for src, dst in zip(src_refs, dst_refs):
        dst[...] = src[...].astype(BF16)


def _gmlp_in_kernel(*refs, n_cast):
    x_ref, g_ref, w_ref = refs[:3]
    cast_src = refs[3:3 + n_cast]
    z_ref = refs[3 + n_cast]
    cast_dst = refs[4 + n_cast:4 + 2 * n_cast]
    hn_ref = refs[4 + 2 * n_cast]

    @pl.when(pl.program_id(1) == 0)
    def _():
        hn_ref[...] = _rmsnorm(x_ref[...], g_ref[...]).astype(BF16)

    z = jnp.dot(hn_ref[...], w_ref[...].astype(BF16), preferred_element_type=F32)
    sqrt_half = np.float32(np.sqrt(0.5))
    z_ref[...] = (0.5 * z * (1.0 + lax.erf(z * sqrt_half))).astype(BF16)
    _cast_ahead(cast_src, cast_dst)


def _gmlp_in(x, g, w, layer, cast, *, tm, tn):
    m, d = x.shape
    n = w.shape[2]
    grid = (m // tm, n // tn)
    c_in, c_out, c_shapes = _cast_ahead_specs(cast, grid[0] * grid[1],
                                              lambda i, j: i * grid[1] + j)
    return pl.pallas_call(
        functools.partial(_gmlp_in_kernel, n_cast=len(cast)),
        grid=grid,
        in_specs=[
            pl.BlockSpec((tm, d), lambda i, j: (i, 0)),
            pl.BlockSpec((1, d), lambda i, j: (0, 0)),
            pl.BlockSpec((None, d, tn), lambda i, j: (layer, 0, j)),
        ] + c_in,
        out_specs=[pl.BlockSpec((tm, tn), lambda i, j: (i, j))] + c_out,
        out_shape=[jax.ShapeDtypeStruct((m, n), BF16)] + c_shapes,
        scratch_shapes=[pltpu.VMEM((tm, d), BF16)],
        compiler_params=pltpu.CompilerParams(
            dimension_semantics=("parallel", "arbitrary"),
            vmem_limit_bytes=VMEM_LIMIT_BYTES),
        name="gmlp_in",
    )(x, g, w, *[cw for cw, _ in cast])


def _gmlp_out_kernel(*refs, n_cast):
    x_ref, u_ref, v_ref, lng_ref, lnb_ref, ws_ref, bst_ref, wout_ref = refs[:8]
    cast_src = refs[8:8 + n_cast]
    o_ref = refs[8 + n_cast]
    cast_dst = refs[9 + n_cast:9 + 2 * n_cast]
    vn_ref, gated_ref = refs[9 + 2 * n_cast:]
    tm, width = vn_ref.shape
    head = width // A_GROUPS

    v = v_ref[...].astype(F32)
    mu = jnp.mean(v, axis=-1, keepdims=True)
    vc = v - mu
    var = jnp.mean(vc * vc, axis=-1, keepdims=True)
    vn_ref[...] = (vc * lax.rsqrt(var + EPS) * lng_ref[...] + lnb_ref[...]).astype(BF16)

    row = lax.broadcasted_iota(jnp.int32, (CHUNK, CHUNK), 0)
    col = lax.broadcasted_iota(jnp.int32, (CHUNK, CHUNK), 1)
    causal = (row >= col).astype(F32)
    for g in range(A_GROUPS):
        cols = slice(g * head, (g + 1) * head)
        w_g = (ws_ref[g] * causal).astype(BF16)
        b_g = bst_ref[:, g:g + 1]
        for c in range(tm // CHUNK):
            rows = slice(c * CHUNK, (c + 1) * CHUNK)
            s = jnp.dot(w_g, vn_ref[rows, cols], preferred_element_type=F32) + b_g
            gated_ref[rows, cols] = (u_ref[rows, cols].astype(F32) * s).astype(BF16)

    o_ref[...] = x_ref[...] + jnp.dot(gated_ref[...], wout_ref[...].astype(BF16),
                                      preferred_element_type=F32)
    _cast_ahead(cast_src, cast_dst)


def _gmlp_out(x, z, ln_g, ln_b, w_s, b_s_t, w_out, cast, *, tm):
    m, d = x.shape
    width = w_out.shape[0]
    c_in, c_out, c_shapes = _cast_ahead_specs(cast, m // tm, lambda i: i)
    return pl.pallas_call(
        functools.partial(_gmlp_out_kernel, n_cast=len(cast)),
        grid=(m // tm,),
        in_specs=[
            pl.BlockSpec((tm, d), lambda i: (i, 0)),
            pl.BlockSpec((tm, width), lambda i: (i, 0)),
            pl.BlockSpec((tm, width), lambda i: (i, 1)),
            _const_spec((1, width)),
            _const_spec((1, width)),
            _const_spec(w_s.shape),
            _const_spec(b_s_t.shape),
            _const_spec(w_out.shape),
        ] + c_in,
        out_specs=[pl.BlockSpec((tm, d), lambda i: (i, 0))] + c_out,
        out_shape=[jax.ShapeDtypeStruct((m, d), F32)] + c_shapes,
        scratch_shapes=[pltpu.VMEM((tm, width), BF16), pltpu.VMEM((tm, width), BF16)],
        compiler_params=pltpu.CompilerParams(
            dimension_semantics=("parallel",),
            vmem_limit_bytes=VMEM_LIMIT_BYTES),
        name="gmlp_out",
    )(x, z, z, ln_g, ln_b, w_s, b_s_t, w_out, *[cw for cw, _ in cast])


def _mlp_kernel(h_ref, g_ref, w1_hbm, w2_hbm, fg_ref, o_ref, hn_ref, w1_buf, w2_buf, sem,
                *, layer, tf, apply_final_norm):
    i = pl.program_id(0)
    w1_src = w1_hbm if layer is None else w1_hbm.at[layer]
    w2_src = w2_hbm if layer is None else w2_hbm.at[layer]
    n_tiles = w1_src.shape[1] // tf
    assert n_tiles % 2 == 0

    def copies(t, slot):
        off = t * tf if isinstance(t, int) else pl.multiple_of(t * tf, tf)
        return (
            pltpu.make_async_copy(w1_src.at[:, pl.ds(off, tf)], w1_buf.at[slot],
                                  sem.at[0, slot]),
            pltpu.make_async_copy(w2_src.at[pl.ds(off, tf), :], w2_buf.at[slot],
                                  sem.at[1, slot]),
        )

    def tile(buf, slot):
        w = buf[slot]
        return w if w.dtype == BF16 else w.astype(BF16)

    def start(t, slot):
        for c in copies(t, slot):
            c.start()

    def wait(t, slot):
        for c in copies(t, slot):
            c.wait()

    @pl.when(i == 0)
    def _():
        start(0, 0)

    h = h_ref[...]
    hn_ref[...] = _rmsnorm(h, g_ref[...]).astype(BF16)
    o_ref[...] = h

    def accumulate(slot):
        a = jnp.dot(hn_ref[...], tile(w1_buf, slot), preferred_element_type=F32)
        a = jnp.maximum(a, 0.0)
        o_ref[...] += jnp.dot((a * a).astype(BF16), tile(w2_buf, slot),
                              preferred_element_type=F32)

    def pair(p, carry):
        t = 2 * p
        start(t + 1, 1)
        wait(t, 0)
        accumulate(0)

        is_last_tile = t + 2 == n_tiles

        @pl.when(jnp.logical_not(jnp.logical_and(is_last_tile, i == pl.num_programs(0) - 1)))
        def _():
            start(jnp.where(is_last_tile, 0, t + 2), 0)

        wait(t + 1, 1)
        accumulate(1)
        return carry

    lax.fori_loop(0, n_tiles // 2, pair, 0)

    if apply_final_norm:
        o_ref[...] = _rmsnorm(o_ref[...], fg_ref[...])


def _mlp(h, g, w1, w2, final_g, layer, *, tm, tf, apply_final_norm):
    m, d = h.shape
    assert (layer is None) == (w1.ndim == 2 and w1.dtype == BF16 and w2.dtype == BF16)
    return pl.pallas_call(
        functools.partial(_mlp_kernel, layer=layer, tf=tf, apply_final_norm=apply_final_norm),
        grid=(m // tm,),
        in_specs=[
            pl.BlockSpec((tm, d), lambda i: (i, 0)),
            pl.BlockSpec((1, d), lambda i: (0, 0)),
            pl.BlockSpec(memory_space=pl.ANY),
            pl.BlockSpec(memory_space=pl.ANY),
            pl.BlockSpec((1, d), lambda i: (0, 0)),
        ],
        out_specs=pl.BlockSpec((tm, d), lambda i: (i, 0)),
        out_shape=jax.ShapeDtypeStruct((m, d), F32),
        scratch_shapes=[
            pltpu.VMEM((tm, d), BF16),
            pltpu.VMEM((2, d, tf), w1.dtype),
            pltpu.VMEM((2, tf, d), w2.dtype),
            pltpu.SemaphoreType.DMA((2, 2)),
        ],
        compiler_params=pltpu.CompilerParams(
            dimension_semantics=("arbitrary",),
            vmem_limit_bytes=VMEM_LIMIT_BYTES),
        name="mlp_final" if apply_final_norm else "mlp",
    )(h, g, w1, w2, final_g)


def _pool_kernel(*refs, seq, n_cast):
    x_ref, g_ref, win_ref, wgrp_ref, scale_ref, wout_ref = refs[:6]
    cast_src = refs[6:6 + n_cast]
    o_ref = refs[6 + n_cast]
    cast_dst = refs[7 + n_cast:7 + 2 * n_cast]
    vbuf_ref, mixed_ref = refs[7 + 2 * n_cast:]
    tm, width = mixed_ref.shape
    head = width // len(B_WINDOWS)
    i = pl.program_id(0)
    start = (i * tm) % seq

    @pl.when(start == 0)
    def _():
        vbuf_ref[0:HALO, :] = jnp.zeros((HALO, width), F32)

    @pl.when(start != 0)
    def _():
        vbuf_ref[0:HALO, :] = vbuf_ref[tm:tm + HALO, :]

    x = x_ref[...]
    hn = _rmsnorm(x, g_ref[...]).astype(BF16)
    vbuf_ref[HALO:HALO + tm, :] = jnp.dot(hn, win_ref[...].astype(BF16),
                                          preferred_element_type=F32)

    pos = start + lax.broadcasted_iota(jnp.int32, (tm, 1), 0)
    for g, w in enumerate(B_WINDOWS):
        cols = slice(g * head, (g + 1) * head)
        v = vbuf_ref[HALO:HALO + tm, cols]
        part = vbuf_ref[:, cols]
        span = 1
        while span < min(w, SUBLANES):
            part = part + pltpu.roll(part, span, axis=0)
            span *= 2
        total = part[HALO:, :]
        if w > SUBLANES:
            assert w == 2 * SUBLANES
            total = total + part[HALO - SUBLANES:HALO - SUBLANES + tm, :]
        inv_count = 1.0 / jnp.minimum(pos + 1, w).astype(F32)
        pooled = (total * inv_count - v).astype(BF16)
        mixed = jnp.dot(pooled, wgrp_ref[g].astype(BF16), preferred_element_type=F32)
        mixed_ref[:, cols] = (mixed * scale_ref[:, cols]).astype(BF16)

    o_ref[...] = x + jnp.dot(mixed_ref[...], wout_ref[...].astype(BF16),
                             preferred_element_type=F32)
    _cast_ahead(cast_src, cast_dst)


def _pool_mixer(x, g, w_in, w_grp, scale, w_out, cast, *, tm, seq):
    m, d = x.shape
    width = w_in.shape[1]
    c_in, c_out, c_shapes = _cast_ahead_specs(cast, m // tm, lambda i: i)
    return pl.pallas_call(
        functools.partial(_pool_kernel, seq=seq, n_cast=len(cast)),
        grid=(m // tm,),
        in_specs=[
            pl.BlockSpec((tm, d), lambda i: (i, 0)),
            _const_spec((1, d)),
            _const_spec(w_in.shape),
            _const_spec(w_grp.shape),
            _const_spec((1, width)),
            _const_spec(w_out.shape),
        ] + c_in,
        out_specs=[pl.BlockSpec((tm, d), lambda i: (i, 0))] + c_out,
        out_shape=[jax.ShapeDtypeStruct((m, d), F32)] + c_shapes,
        scratch_shapes=[pltpu.VMEM((HALO + tm, width), F32), pltpu.VMEM((tm, width), BF16)],
        compiler_params=pltpu.CompilerParams(
            dimension_semantics=("arbitrary",),
            vmem_limit_bytes=VMEM_LIMIT_BYTES),
        name="pool_mixer",
    )(x, g, w_in, w_grp, scale, w_out, *[cw for cw, _ in cast])


def kernel(x, a_w_in, a_ln_g, a_ln_b, a_w_s, a_b_s, a_w_out, b_w_in, b_w_grp, b_scale, b_w_out,
           norm_mix, norm_mlp, mlp_w1, mlp_w2, final_norm):
    bsz, seq, d = x.shape
    depth = norm_mix.shape[0]
    assert depth == 2
    h = x.reshape(bsz * seq, d)
    row = lambda p: p.reshape(1, -1)

    z, = _gmlp_in(h, row(norm_mix[0]), a_w_in, 0, [], tm=2048, tn=512)
    h, = _gmlp_out(h, z, row(a_ln_g[0]), row(a_ln_b[0]), a_w_s[0], a_b_s[0].T, a_w_out[0], [],
                   tm=512)
    h = _mlp(h, row(norm_mlp[0]), mlp_w1, mlp_w2, row(final_norm), 0, tm=1024, tf=512,
             apply_final_norm=False)

    h, w1_1, w2_1 = _pool_mixer(h, row(norm_mix[1]), b_w_in[0], b_w_grp[0], row(b_scale[0]),
                                b_w_out[0], [(mlp_w1, 1), (mlp_w2, 1)], tm=256, seq=seq)
    h = _mlp(h, row(norm_mlp[1]), w1_1, w2_1, row(final_norm), None, tm=1024, tf=1024,
             apply_final_norm=True)
    return h.reshape(bsz, seq, d)
```

```python
import functools

import numpy as np
import jax
import jax.numpy as jnp
from jax import lax
from jax.experimental import pallas as pl
from jax.experimental.pallas import tpu as pltpu

EPS = 1e-6
CHUNK = 128
A_GROUPS = 8
B_WINDOWS = (2, 4, 8, 16)
SUBLANES = 8
BF16_SUBLANES = 16
HALO = 16

VMEM_LIMIT_BYTES = 60 * 1024 * 1024

F32 = jnp.float32
BF16 = jnp.bfloat16


def _rmsnorm(x, g):
    return x * lax.rsqrt(jnp.mean(x * x, axis=-1, keepdims=True) + EPS) * g


def _const_spec(shape):
    return pl.BlockSpec(shape, lambda *_: (0,) * len(shape), pipeline_mode=pl.Buffered(1))


def _cast_ahead_specs(weights, n_steps, step_of):
    in_specs, out_specs, out_shapes = [], [], []
    for w, layer in weights:
        _, rows, cols = w.shape
        assert rows % (n_steps * BF16_SUBLANES) == 0
        rb = rows // n_steps
        in_specs.append(pl.BlockSpec((None, rb, cols),
                                     lambda *idx, layer=layer: (layer, step_of(*idx), 0)))
        out_specs.append(pl.BlockSpec((rb, cols), lambda *idx: (step_of(*idx), 0)))
        out_shapes.append(jax.ShapeDtypeStruct((rows, cols), BF16))
    return in_specs, out_specs, out_shapes


def _cast_ahead(src_refs, dst_refs):
    for src, dst in zip(src_refs, dst_refs):
        dst[...] = src[...].astype(BF16)


def _gmlp_in_kernel(x_ref, g_ref, w_ref, z_ref, hn_ref):
    @pl.when(pl.program_id(1) == 0)
    def _():
        hn_ref[...] = _rmsnorm(x_ref[...], g_ref[...]).astype(BF16)

    z = jnp.dot(hn_ref[...], w_ref[...].astype(BF16), preferred_element_type=F32)
    sqrt_half = np.float32(np.sqrt(0.5))
    z_ref[...] = (0.5 * z * (1.0 + lax.erf(z * sqrt_half))).astype(BF16)


def _gmlp_in(x, g, w, layer, *, tm, tn):
    m, d = x.shape
    n = w.shape[2]
    return pl.pallas_call(
        _gmlp_in_kernel,
        grid=(m // tm, n // tn),
        in_specs=[
            pl.BlockSpec((tm, d), lambda i, j: (i, 0)),
            pl.BlockSpec((1, d), lambda i, j: (0, 0)),
            pl.BlockSpec((None, d, tn), lambda i, j: (layer, 0, j)),
        ],
        out_specs=pl.BlockSpec((tm, tn), lambda i, j: (i, j)),
        out_shape=jax.ShapeDtypeStruct((m, n), BF16),
        scratch_shapes=[pltpu.VMEM((tm, d), BF16)],
        compiler_params=pltpu.CompilerParams(
            dimension_semantics=("parallel", "arbitrary"),
            vmem_limit_bytes=VMEM_LIMIT_BYTES),
        name="gmlp_in",
    )(x, g, w)


def _gmlp_out_kernel(*refs, n_cast):
    x_ref, u_ref, v_ref, lng_ref, lnb_ref, ws_ref, bst_ref, wout_ref = refs[:8]
    cast_src = refs[8:8 + n_cast]
    o_ref = refs[8 + n_cast]
    cast_dst = refs[9 + n_cast:9 + 2 * n_cast]
    vn_ref, gated_ref = refs[9 + 2 * n_cast:]
    tm, width = vn_ref.shape
    head = width // A_GROUPS

    v = v_ref[...].astype(F32)
    mu = jnp.mean(v, axis=-1, keepdims=True)
    vc = v - mu
    var = jnp.mean(vc * vc, axis=-1, keepdims=True)
    vn_ref[...] = (vc * lax.rsqrt(var + EPS) * lng_ref[...] + lnb_ref[...]).astype(BF16)

    row = lax.broadcasted_iota(jnp.int32, (CHUNK, CHUNK), 0)
    col = lax.broadcasted_iota(jnp.int32, (CHUNK, CHUNK), 1)
    causal = (row >= col).astype(F32)
    for g in range(A_GROUPS):
        cols = slice(g * head, (g + 1) * head)
        w_g = (ws_ref[g] * causal).astype(BF16)
        b_g = bst_ref[:, g:g + 1]
        for c in range(tm // CHUNK):
            rows = slice(c * CHUNK, (c + 1) * CHUNK)
            s = jnp.dot(w_g, vn_ref[rows, cols], preferred_element_type=F32) + b_g
            gated_ref[rows, cols] = (u_ref[rows, cols].astype(F32) * s).astype(BF16)

    o_ref[...] = x_ref[...] + jnp.dot(gated_ref[...], wout_ref[...].astype(BF16),
                                      preferred_element_type=F32)
    _cast_ahead(cast_src, cast_dst)


def _gmlp_out(x, z, ln_g, ln_b, w_s, b_s_t, w_out, cast, *, tm):
    m, d = x.shape
    width = w_out.shape[0]
    c_in, c_out, c_shapes = _cast_ahead_specs(cast, m // tm, lambda i: i)
    return pl.pallas_call(
        functools.partial(_gmlp_out_kernel, n_cast=len(cast)),
        grid=(m // tm,),
        in_specs=[
            pl.BlockSpec((tm, d), lambda i: (i, 0)),
            pl.BlockSpec((tm, width), lambda i: (i, 0)),
            pl.BlockSpec((tm, width), lambda i: (i, 1)),
            _const_spec((1, width)),
            _const_spec((1, width)),
            _const_spec(w_s.shape),
            _const_spec(b_s_t.shape),
            _const_spec(w_out.shape),
        ] + c_in,
        out_specs=[pl.BlockSpec((tm, d), lambda i: (i, 0))] + c_out,
        out_shape=[jax.ShapeDtypeStruct((m, d), F32)] + c_shapes,
        scratch_shapes=[pltpu.VMEM((tm, width), BF16), pltpu.VMEM((tm, width), BF16)],
        compiler_params=pltpu.CompilerParams(
            dimension_semantics=("parallel",),
            vmem_limit_bytes=VMEM_LIMIT_BYTES),
        name="gmlp_out",
    )(x, z, z, ln_g, ln_b, w_s, b_s_t, w_out, *[cw for cw, _ in cast])


def _mlp_kernel(h_ref, g_ref, w1_hbm, w2_hbm, fg_ref, o_ref, hn_ref, w1_buf, w2_buf, sem,
                *, layer, tf, apply_final_norm):
    i = pl.program_id(0)
    w1_src = w1_hbm if layer is None else w1_hbm.at[layer]
    w2_src = w2_hbm if layer is None else w2_hbm.at[layer]
    n_tiles = w1_src.shape[1] // tf
    assert n_tiles % 2 == 0

    def copies(t, slot):
        off = t * tf if isinstance(t, int) else pl.multiple_of(t * tf, tf)
        return (
            pltpu.make_async_copy(w1_src.at[:, pl.ds(off, tf)], w1_buf.at[slot],
                                  sem.at[0, slot]),
            pltpu.make_async_copy(w2_src.at[pl.ds(off, tf), :], w2_buf.at[slot],
                                  sem.at[1, slot]),
        )

    def tile(buf, slot):
        w = buf[slot]
        return w if w.dtype == BF16 else w.astype(BF16)

    def start(t, slot):
        for c in copies(t, slot):
            c.start()

    def wait(t, slot):
        for c in copies(t, slot):
            c.wait()

    @pl.when(i == 0)
    def _():
        start(0, 0)

    h = h_ref[...]
    hn_ref[...] = _rmsnorm(h, g_ref[...]).astype(BF16)
    o_ref[...] = h

    def accumulate(slot):
        a = jnp.dot(hn_ref[...], tile(w1_buf, slot), preferred_element_type=F32)
        a = jnp.maximum(a, 0.0)
        o_ref[...] += jnp.dot((a * a).astype(BF16), tile(w2_buf, slot),
                              preferred_element_type=F32)

    def pair(p, carry):
        t = 2 * p
        start(t + 1, 1)
        wait(t, 0)
        accumulate(0)

        is_last_tile = t + 2 == n_tiles

        @pl.when(jnp.logical_not(jnp.logical_and(is_last_tile, i == pl.num_programs(0) - 1)))
        def _():
            start(jnp.where(is_last_tile, 0, t + 2), 0)

        wait(t + 1, 1)
        accumulate(1)
        return carry

    lax.fori_loop(0, n_tiles // 2, pair, 0)

    if apply_final_norm:
        o_ref[...] = _rmsnorm(o_ref[...], fg_ref[...])


def _mlp(h, g, w1, w2, final_g, layer, *, tm, tf, apply_final_norm):
    m, d = h.shape
    assert (layer is None) == (w1.ndim == 2 and w1.dtype == BF16 and w2.dtype == BF16)
    return pl.pallas_call(
        functools.partial(_mlp_kernel, layer=layer, tf=tf, apply_final_norm=apply_final_norm),
        grid=(m // tm,),
        in_specs=[
            pl.BlockSpec((tm, d), lambda i: (i, 0)),
            pl.BlockSpec((1, d), lambda i: (0, 0)),
            pl.BlockSpec(memory_space=pl.ANY),
            pl.BlockSpec(memory_space=pl.ANY),
            pl.BlockSpec((1, d), lambda i: (0, 0)),
        ],
        out_specs=pl.BlockSpec((tm, d), lambda i: (i, 0)),
        out_shape=jax.ShapeDtypeStruct((m, d), F32),
        scratch_shapes=[
            pltpu.VMEM((tm, d), BF16),
            pltpu.VMEM((2, d, tf), w1.dtype),
            pltpu.VMEM((2, tf, d), w2.dtype),
            pltpu.SemaphoreType.DMA((2, 2)),
        ],
        compiler_params=pltpu.CompilerParams(
            dimension_semantics=("arbitrary",),
            vmem_limit_bytes=VMEM_LIMIT_BYTES),
        name="mlp_final" if apply_final_norm else "mlp",
    )(h, g, w1, w2, final_g)


def _pool_kernel(*refs, seq, n_cast):
    x_ref, g_ref, win_ref, wgrp_ref, scale_ref, wout_ref = refs[:6]
    cast_src = refs[6:6 + n_cast]
    o_ref = refs[6 + n_cast]
    cast_dst = refs[7 + n_cast:7 + 2 * n_cast]
    vbuf_ref, mixed_ref = refs[7 + 2 * n_cast:]
    tm, width = mixed_ref.shape
    head = width // len(B_WINDOWS)
    i = pl.program_id(0)
    start = (i * tm) % seq

    @pl.when(start == 0)
    def _():
        vbuf_ref[0:HALO, :] = jnp.zeros((HALO, width), F32)

    @pl.when(start != 0)
    def _():
        vbuf_ref[0:HALO, :] = vbuf_ref[tm:tm + HALO, :]

    x = x_ref[...]
    hn = _rmsnorm(x, g_ref[...]).astype(BF16)
    vbuf_ref[HALO:HALO + tm, :] = jnp.dot(hn, win_ref[...], preferred_element_type=F32)

    pos = start + lax.broadcasted_iota(jnp.int32, (tm, 1), 0)
    for g, w in enumerate(B_WINDOWS):
        cols = slice(g * head, (g + 1) * head)
        v = vbuf_ref[HALO:HALO + tm, cols]
        part = vbuf_ref[:, cols]
        span = 1
        while span < min(w, SUBLANES):
            part = part + pltpu.roll(part, span, axis=0)
            span *= 2
        total = part[HALO:, :]
        if w > SUBLANES:
            assert w == 2 * SUBLANES
            total = total + part[HALO - SUBLANES:HALO - SUBLANES + tm, :]
        inv_count = 1.0 / jnp.minimum(pos + 1, w).astype(F32)
        pooled = (total * inv_count - v).astype(BF16)
        mixed = jnp.dot(pooled, wgrp_ref[g], preferred_element_type=F32)
        mixed_ref[:, cols] = (mixed * scale_ref[:, cols]).astype(BF16)

    o_ref[...] = x + jnp.dot(mixed_ref[...], wout_ref[...], preferred_element_type=F32)
    _cast_ahead(cast_src, cast_dst)


def _pool_mixer(x, g, w_in, w_grp, scale, w_out, cast, *, tm, seq):
    m, d = x.shape
    assert w_in.dtype == BF16 and w_grp.dtype == BF16 and w_out.dtype == BF16
    width = w_in.shape[1]
    c_in, c_out, c_shapes = _cast_ahead_specs(cast, m // tm, lambda i: i)
    return pl.pallas_call(
        functools.partial(_pool_kernel, seq=seq, n_cast=len(cast)),
        grid=(m // tm,),
        in_specs=[
            pl.BlockSpec((tm, d), lambda i: (i, 0)),
            _const_spec((1, d)),
            _const_spec(w_in.shape),
            _const_spec(w_grp.shape),
            _const_spec((1, width)),
            _const_spec(w_out.shape),
        ] + c_in,
        out_specs=[pl.BlockSpec((tm, d), lambda i: (i, 0))] + c_out,
        out_shape=[jax.ShapeDtypeStruct((m, d), F32)] + c_shapes,
        scratch_shapes=[pltpu.VMEM((HALO + tm, width), F32), pltpu.VMEM((tm, width), BF16)],
        compiler_params=pltpu.CompilerParams(
            dimension_semantics=("arbitrary",),
            vmem_limit_bytes=VMEM_LIMIT_BYTES),
        name="pool_mixer",
    )(x, g, w_in, w_grp, scale, w_out, *[cw for cw, _ in cast])


def kernel(x, a_w_in, a_ln_g, a_ln_b, a_w_s, a_b_s, a_w_out, b_w_in, b_w_grp, b_scale, b_w_out,
           norm_mix, norm_mlp, mlp_w1, mlp_w2, final_norm):
    bsz, seq, d = x.shape
    depth = norm_mix.shape[0]
    assert depth == 2
    h = x.reshape(bsz * seq, d)
    row = lambda p: p.reshape(1, -1)

    z = _gmlp_in(h, row(norm_mix[0]), a_w_in, 0, tm=2048, tn=512)
    n_grp, head, _ = b_w_grp.shape[1:]
    h, p_in, p_grp, p_out = _gmlp_out(
        h, z, row(a_ln_g[0]), row(a_ln_b[0]), a_w_s[0], a_b_s[0].T, a_w_out[0],
        [(b_w_in, 0), (b_w_grp.reshape(-1, n_grp * head, head), 0), (b_w_out, 0)], tm=512)
    h = _mlp(h, row(norm_mlp[0]), mlp_w1, mlp_w2, row(final_norm), 0, tm=1024, tf=512,
             apply_final_norm=False)

    h, w1_1, w2_1 = _pool_mixer(h, row(norm_mix[1]), p_in, p_grp.reshape(n_grp, head, head),
                                row(b_scale[0]), p_out, [(mlp_w1, 1), (mlp_w2, 1)],
                                tm=256, seq=seq)
    h = _mlp(h, row(norm_mlp[1]), w1_1, w2_1, row(final_norm), None, tm=1024, tf=1024,
             apply_final_norm=True)
    return h.reshape(bsz, seq, d)
```

```python
import functools

import numpy as np
import jax
import jax.numpy as jnp
from jax import lax
from jax.experimental import pallas as pl
from jax.experimental.pallas import tpu as pltpu

EPS = 1e-6
CHUNK = 128
A_GROUPS = 8
B_WINDOWS = (2, 4, 8, 16)
SUBLANES = 8
BF16_SUBLANES = 16
HALO = 16

VMEM_LIMIT_BYTES = 60 * 1024 * 1024

F32 = jnp.float32
BF16 = jnp.bfloat16


def _rmsnorm(x, g):
    return x * lax.rsqrt(jnp.mean(x * x, axis=-1, keepdims=True) + EPS) * g


def _const_spec(shape):
    return pl.BlockSpec(shape, lambda *_: (0,) * len(shape), pipeline_mode=pl.Buffered(1))


def _cast_ahead_specs(weights, n_steps, step_of):
    in_specs, out_specs, out_shapes = [], [], []
    for w, layer in weights:
        _, rows, cols = w.shape
        assert rows % (n_steps * BF16_SUBLANES) == 0
        rb = rows // n_steps
        in_specs.append(pl.BlockSpec((None, rb, cols),
                                     lambda *idx, layer=layer: (layer, step_of(*idx), 0)))
        out_specs.append(pl.BlockSpec((rb, cols), lambda *idx: (step_of(*idx), 0)))
        out_shapes.append(jax.ShapeDtypeStruct((rows, cols), BF16))
    return in_specs, out_specs, out_shapes


def _cast_ahead(src_refs, dst_refs):
    for src, dst in zip(src_refs, dst_refs):
        dst[...] = src[...].astype(BF16)


def _gmlp_in_kernel(*refs, n_cast):
    x_ref, g_ref, w_ref = refs[:3]
    cast_src = refs[3:3 + n_cast]
    z_ref = refs[3 + n_cast]
    cast_dst = refs[4 + n_cast:4 + 2 * n_cast]
    hn_ref = refs[4 + 2 * n_cast]

    @pl.when(pl.program_id(1) == 0)
    def _():
        hn_ref[...] = _rmsnorm(x_ref[...], g_ref[...]).astype(BF16)

    z = jnp.dot(hn_ref[...], w_ref[...].astype(BF16), preferred_element_type=F32)
    sqrt_half = np.float32(np.sqrt(0.5))
    z_ref[...] = (0.5 * z * (1.0 + lax.erf(z * sqrt_half))).astype(BF16)
    _cast_ahead(cast_src, cast_dst)


def _gmlp_in(x, g, w, layer, cast, *, tm, tn):
    m, d = x.shape
    n = w.shape[2]
    grid = (m // tm, n // tn)
    c_in, c_out, c_shapes = _cast_ahead_specs(cast, grid[0] * grid[1],
                                              lambda i, j: i * grid[1] + j)
    return pl.pallas_call(
        functools.partial(_gmlp_in_kernel, n_cast=len(cast)),
        grid=grid,
        in_specs=[
            pl.BlockSpec((tm, d), lambda i, j: (i, 0)),
            pl.BlockSpec((1, d), lambda i, j: (0, 0)),
            pl.BlockSpec((None, d, tn), lambda i, j: (layer, 0, j)),
        ] + c_in,
        out_specs=[pl.BlockSpec((tm, tn), lambda i, j: (i, j))] + c_out,
        out_shape=[jax.ShapeDtypeStruct((m, n), BF16)] + c_shapes,
        scratch_shapes=[pltpu.VMEM((tm, d), BF16)],
        compiler_params=pltpu.CompilerParams(
            dimension_semantics=("parallel", "arbitrary"),
            vmem_limit_bytes=VMEM_LIMIT_BYTES),
        name="gmlp_in",
    )(x, g, w, *[cw for cw, _ in cast])


def _gmlp_out_kernel(*refs, n_cast):
    x_ref, u_ref, v_ref, lng_ref, lnb_ref, ws_ref, bst_ref, wout_ref = refs[:8]
    cast_src = refs[8:8 + n_cast]
    o_ref = refs[8 + n_cast]
    cast_dst = refs[9 + n_cast:9 + 2 * n_cast]
    vn_ref, gated_ref = refs[9 + 2 * n_cast:]
    tm, width = vn_ref.shape
    head = width // A_GROUPS

    v = v_ref[...].astype(F32)
    mu = jnp.mean(v, axis=-1, keepdims=True)
    vc = v - mu
    var = jnp.mean(vc * vc, axis=-1, keepdims=True)
    vn_ref[...] = (vc * lax.rsqrt(var + EPS) * lng_ref[...] + lnb_ref[...]).astype(BF16)

    row = lax.broadcasted_iota(jnp.int32, (CHUNK, CHUNK), 0)
    col = lax.broadcasted_iota(jnp.int32, (CHUNK, CHUNK), 1)
    causal = (row >= col).astype(F32)
    for g in range(A_GROUPS):
        cols = slice(g * head, (g + 1) * head)
        w_g = (ws_ref[g] * causal).astype(BF16)
        b_g = bst_ref[:, g:g + 1]
        for c in range(tm // CHUNK):
            rows = slice(c * CHUNK, (c + 1) * CHUNK)
            s = jnp.dot(w_g, vn_ref[rows, cols], preferred_element_type=F32) + b_g
            gated_ref[rows, cols] = (u_ref[rows, cols].astype(F32) * s).astype(BF16)

    o_ref[...] = x_ref[...] + jnp.dot(gated_ref[...], wout_ref[...].astype(BF16),
                                      preferred_element_type=F32)
    _cast_ahead(cast_src, cast_dst)


def _gmlp_out(x, z, ln_g, ln_b, w_s, b_s_t, w_out, cast, *, tm):
    m, d = x.shape
    width = w_out.shape[0]
    c_in, c_out, c_shapes = _cast_ahead_specs(cast, m // tm, lambda i: i)
    return pl.pallas_call(
        functools.partial(_gmlp_out_kernel, n_cast=len(cast)),
        grid=(m // tm,),
        in_specs=[
            pl.BlockSpec((tm, d), lambda i: (i, 0)),
            pl.BlockSpec((tm, width), lambda i: (i, 0)),
            pl.BlockSpec((tm, width), lambda i: (i, 1)),
            _const_spec((1, width)),
            _const_spec((1, width)),
            _const_spec(w_s.shape),
            _const_spec(b_s_t.shape),
            _const_spec(w_out.shape),
        ] + c_in,
        out_specs=[pl.BlockSpec((tm, d), lambda i: (i, 0))] + c_out,
        out_shape=[jax.ShapeDtypeStruct((m, d), F32)] + c_shapes,
        scratch_shapes=[pltpu.VMEM((tm, width), BF16), pltpu.VMEM((tm, width), BF16)],
        compiler_params=pltpu.CompilerParams(
            dimension_semantics=("parallel",),
            vmem_limit_bytes=VMEM_LIMIT_BYTES),
        name="gmlp_out",
    )(x, z, z, ln_g, ln_b, w_s, b_s_t, w_out, *[cw for cw, _ in cast])


def _mlp_kernel(h_hbm, g_ref, w1_hbm, w2_hbm, fg_ref, o_ref, h_buf, hn_ref, w1_buf, w2_buf, sem, h_sem,
                *, layers, tf, apply_final_norm):
    i = pl.program_id(0)
    tm = h_buf.shape[0]
    w1_src = w1_hbm if layers[0] is None else w1_hbm.at[layers[0]]
    w2_src = w2_hbm if layers[1] is None else w2_hbm.at[layers[1]]

    def h_copy(tile_index):
        first = tile_index * tm
        if not isinstance(tile_index, int):
            first = pl.multiple_of(first, tm)
        return pltpu.make_async_copy(h_hbm.at[pl.ds(first, tm), :], h_buf, h_sem.at[0])

    n_tiles = w1_src.shape[1] // tf
    assert n_tiles % 2 == 0

    def copies(t, slot):
        off = t * tf if isinstance(t, int) else pl.multiple_of(t * tf, tf)
        return (
            pltpu.make_async_copy(w1_src.at[:, pl.ds(off, tf)], w1_buf.at[slot],
                                  sem.at[0, slot]),
            pltpu.make_async_copy(w2_src.at[pl.ds(off, tf), :], w2_buf.at[slot],
                                  sem.at[1, slot]),
        )

    def tile(buf, slot):
        w = buf[slot]
        return w if w.dtype == BF16 else w.astype(BF16)

    def start(t, slot):
        for c in copies(t, slot):
            c.start()

    def wait(t, slot):
        for c in copies(t, slot):
            c.wait()

    @pl.when(i == 0)
    def _():
        h_copy(0).start()
        start(0, 0)

    h_copy(i).wait()
    h = h_buf[...]
    hn_ref[...] = _rmsnorm(h, g_ref[...]).astype(BF16)
    o_ref[...] = h

    @pl.when(i + 1 < pl.num_programs(0))
    def _():
        h_copy(i + 1).start()

    def accumulate(slot):
        a = jnp.dot(hn_ref[...], tile(w1_buf, slot), preferred_element_type=F32)
        a = jnp.maximum(a, 0.0)
        o_ref[...] += jnp.dot((a * a).astype(BF16), tile(w2_buf, slot),
                              preferred_element_type=F32)

    def pair(p, carry):
        t = 2 * p
        start(t + 1, 1)
        wait(t, 0)
        accumulate(0)

        is_last_tile = t + 2 == n_tiles

        @pl.when(jnp.logical_not(jnp.logical_and(is_last_tile, i == pl.num_programs(0) - 1)))
        def _():
            start(jnp.where(is_last_tile, 0, t + 2), 0)

        wait(t + 1, 1)
        accumulate(1)
        return carry

    lax.fori_loop(0, n_tiles // 2, pair, 0)

    if apply_final_norm:
        o_ref[...] = _rmsnorm(o_ref[...], fg_ref[...])


def _mlp(h, g, w1, w2, final_g, layers, *, tm, tf, apply_final_norm):
    m, d = h.shape
    for w, layer in zip((w1, w2), layers):
        assert (layer is None) == (w.ndim == 2 and w.dtype == BF16)
    return pl.pallas_call(
        functools.partial(_mlp_kernel, layers=layers, tf=tf, apply_final_norm=apply_final_norm),
        grid=(m // tm,),
        in_specs=[
            pl.BlockSpec(memory_space=pl.ANY),
            pl.BlockSpec((1, d), lambda i: (0, 0)),
            pl.BlockSpec(memory_space=pl.ANY),
            pl.BlockSpec(memory_space=pl.ANY),
            pl.BlockSpec((1, d), lambda i: (0, 0)),
        ],
        out_specs=pl.BlockSpec((tm, d), lambda i: (i, 0)),
        out_shape=jax.ShapeDtypeStruct((m, d), F32),
        scratch_shapes=[
            pltpu.VMEM((tm, d), F32),
            pltpu.VMEM((tm, d), BF16),
            pltpu.VMEM((2, d, tf), w1.dtype),
            pltpu.VMEM((2, tf, d), w2.dtype),
            pltpu.SemaphoreType.DMA((2, 2)),
            pltpu.SemaphoreType.DMA((1,)),
        ],
        compiler_params=pltpu.CompilerParams(
            dimension_semantics=("arbitrary",),
            vmem_limit_bytes=VMEM_LIMIT_BYTES),
        name="mlp_final" if apply_final_norm else "mlp",
    )(h, g, w1, w2, final_g)


def _pool_kernel(*refs, seq, n_cast):
    x_ref, g_ref, win_ref, wgrp_ref, scale_ref, wout_ref = refs[:6]
    cast_src = refs[6:6 + n_cast]
    o_ref = refs[6 + n_cast]
    cast_dst = refs[7 + n_cast:7 + 2 * n_cast]
    vbuf_ref, mixed_ref = refs[7 + 2 * n_cast:]
    tm, width = mixed_ref.shape
    head = width // len(B_WINDOWS)
    i = pl.program_id(0)
    start = (i * tm) % seq

    @pl.when(start == 0)
    def _():
        vbuf_ref[0:HALO, :] = jnp.zeros((HALO, width), F32)

    @pl.when(start != 0)
    def _():
        vbuf_ref[0:HALO, :] = vbuf_ref[tm:tm + HALO, :]

    x = x_ref[...]
    hn = _rmsnorm(x, g_ref[...]).astype(BF16)
    vbuf_ref[HALO:HALO + tm, :] = jnp.dot(hn, win_ref[...].astype(BF16),
                                          preferred_element_type=F32)

    pos = start + lax.broadcasted_iota(jnp.int32, (tm, 1), 0)
    for g, w in enumerate(B_WINDOWS):
        cols = slice(g * head, (g + 1) * head)
        v = vbuf_ref[HALO:HALO + tm, cols]
        part = vbuf_ref[:, cols]
        span = 1
        while span < min(w, SUBLANES):
            part = part + pltpu.roll(part, span, axis=0)
            span *= 2
        total = part[HALO:, :]
        if w > SUBLANES:
            assert w == 2 * SUBLANES
            total = total + part[HALO - SUBLANES:HALO - SUBLANES + tm, :]
        inv_count = 1.0 / jnp.minimum(pos + 1, w).astype(F32)
        pooled = (total * inv_count - v).astype(BF16)
        mixed = jnp.dot(pooled, wgrp_ref[g].astype(BF16), preferred_element_type=F32)
        mixed_ref[:, cols] = (mixed * scale_ref[:, cols]).astype(BF16)

    o_ref[...] = x + jnp.dot(mixed_ref[...], wout_ref[...].astype(BF16),
                             preferred_element_type=F32)
    _cast_ahead(cast_src, cast_dst)


def _pool_mixer(x, g, w_in, w_grp, scale, w_out, cast, *, tm, seq):
    m, d = x.shape
    width = w_in.shape[1]
    c_in, c_out, c_shapes = _cast_ahead_specs(cast, m // tm, lambda i: i)
    return pl.pallas_call(
        functools.partial(_pool_kernel, seq=seq, n_cast=len(cast)),
        grid=(m // tm,),
        in_specs=[
            pl.BlockSpec((tm, d), lambda i: (i, 0)),
            _const_spec((1, d)),
            _const_spec(w_in.shape),
            _const_spec(w_grp.shape),
            _const_spec((1, width)),
            _const_spec(w_out.shape),
        ] + c_in,
        out_specs=[pl.BlockSpec((tm, d), lambda i: (i, 0))] + c_out,
        out_shape=[jax.ShapeDtypeStruct((m, d), F32)] + c_shapes,
        scratch_shapes=[pltpu.VMEM((HALO + tm, width), F32), pltpu.VMEM((tm, width), BF16)],
        compiler_params=pltpu.CompilerParams(
            dimension_semantics=("arbitrary",),
            vmem_limit_bytes=VMEM_LIMIT_BYTES),
        name="pool_mixer",
    )(x, g, w_in, w_grp, scale, w_out, *[cw for cw, _ in cast])


def kernel(x, a_w_in, a_ln_g, a_ln_b, a_w_s, a_b_s, a_w_out, b_w_in, b_w_grp, b_scale, b_w_out,
           norm_mix, norm_mlp, mlp_w1, mlp_w2, final_norm):
    bsz, seq, d = x.shape
    depth = norm_mix.shape[0]
    assert depth == 2
    h = x.reshape(bsz * seq, d)
    row = lambda p: p.reshape(1, -1)

    z, = _gmlp_in(h, row(norm_mix[0]), a_w_in, 0, [], tm=2048, tn=512)
    h, w2_0 = _gmlp_out(h, z, row(a_ln_g[0]), row(a_ln_b[0]), a_w_s[0], a_b_s[0].T, a_w_out[0],
                        [(mlp_w2, 0)], tm=512)
    h = _mlp(h, row(norm_mlp[0]), mlp_w1, w2_0, row(final_norm), (0, None), tm=1024, tf=1024,
             apply_final_norm=False)

    h, w1_1, w2_1 = _pool_mixer(h, row(norm_mix[1]), b_w_in[0], b_w_grp[0], row(b_scale[0]),
                                b_w_out[0], [(mlp_w1, 1), (mlp_w2, 1)], tm=256, seq=seq)
    h = _mlp(h, row(norm_mlp[1]), w1_1, w2_1, row(final_norm), (None, None), tm=1024, tf=1024,
             apply_final_norm=True)
    return h.reshape(bsz, seq, d)
```

```python
import functools

import numpy as np
import jax
import jax.numpy as jnp
from jax import lax
from jax.experimental import pallas as pl
from jax.experimental.pallas import tpu as pltpu

EPS = 1e-6
CHUNK = 128
A_GROUPS = 8
B_WINDOWS = (2, 4, 8, 16)
SUBLANES = 8
BF16_SUBLANES = 16
HALO = 16

VMEM_LIMIT_BYTES = 60 * 1024 * 1024
VMEM_LIMIT_EXTRA_BYTES = 1024 * 1024

F32 = jnp.float32
BF16 = jnp.bfloat16


def _rmsnorm(x, g):
    return x * lax.rsqrt(jnp.mean(x * x, axis=-1, keepdims=True) + EPS) * g


def _const_spec(shape):
    return pl.BlockSpec(shape, lambda *_: (0,) * len(shape), pipeline_mode=pl.Buffered(1))


def _cast_ahead_specs(weights, n_steps, step_of):
    in_specs, out_specs, out_shapes = [], [], []
    for w, layer in weights:
        _, rows, cols = w.shape
        assert rows % (n_steps * BF16_SUBLANES) == 0
        rb = rows // n_steps
        in_specs.append(pl.BlockSpec((None, rb, cols),
                                     lambda *idx, layer=layer: (layer, step_of(*idx), 0)))
        out_specs.append(pl.BlockSpec((rb, cols), lambda *idx: (step_of(*idx), 0)))
        out_shapes.append(jax.ShapeDtypeStruct((rows, cols), BF16))
    return in_specs, out_specs, out_shapes


def _cast_ahead(src_refs, dst_refs):
    for src, dst in zip(src_refs, dst_refs):
        dst[...] = src[...].astype(BF16)


def _gmlp_in_kernel(*refs, n_cast):
    x_hbm, g_ref, w_ref = refs[:3]
    cast_src = refs[3:3 + n_cast]
    z_ref = refs[3 + n_cast]
    cast_dst = refs[4 + n_cast:4 + 2 * n_cast]
    hn_ref, x_buf, x_sem = refs[4 + 2 * n_cast:]
    i = pl.program_id(0)
    tm = x_buf.shape[0]

    def x_copy(tile_index):
        first = tile_index * tm
        if not isinstance(tile_index, int):
            first = pl.multiple_of(first, tm)
        return pltpu.make_async_copy(x_hbm.at[pl.ds(first, tm), :], x_buf, x_sem.at[0])

    @pl.when(pl.program_id(1) == 0)
    def _():
        @pl.when(i == 0)
        def _():
            x_copy(0).start()

        x_copy(i).wait()
        hn_ref[...] = _rmsnorm(x_buf[...], g_ref[...]).astype(BF16)

        @pl.when(i + 1 < pl.num_programs(0))
        def _():
            x_copy(i + 1).start()

    z = jnp.dot(hn_ref[...], w_ref[...].astype(BF16), preferred_element_type=F32)
    sqrt_half = np.float32(np.sqrt(0.5))
    z_ref[...] = (0.5 * z * (1.0 + lax.erf(z * sqrt_half))).astype(BF16)
    _cast_ahead(cast_src, cast_dst)


def _gmlp_in(x, g, w, layer, cast, *, tm, tn):
    m, d = x.shape
    n = w.shape[2]
    grid = (m // tm, n // tn)
    c_in, c_out, c_shapes = _cast_ahead_specs(cast, grid[0] * grid[1],
                                              lambda i, j: i * grid[1] + j)
    return pl.pallas_call(
        functools.partial(_gmlp_in_kernel, n_cast=len(cast)),
        grid=grid,
        in_specs=[
            pl.BlockSpec(memory_space=pl.ANY),
            pl.BlockSpec((1, d), lambda i, j: (0, 0)),
            pl.BlockSpec((None, d, tn), lambda i, j: (layer, 0, j)),
        ] + c_in,
        out_specs=[pl.BlockSpec((tm, tn), lambda i, j: (i, j))] + c_out,
        out_shape=[jax.ShapeDtypeStruct((m, n), BF16)] + c_shapes,
        scratch_shapes=[pltpu.VMEM((tm, d), BF16), pltpu.VMEM((tm, d), F32),
                        pltpu.SemaphoreType.DMA((1,))],
        compiler_params=pltpu.CompilerParams(
            dimension_semantics=("arbitrary", "arbitrary"),
            vmem_limit_bytes=VMEM_LIMIT_BYTES + VMEM_LIMIT_EXTRA_BYTES),
        name="gmlp_in",
    )(x, g, w, *[cw for cw, _ in cast])


def _gmlp_out_kernel(*refs, n_cast):
    x_ref, u_ref, v_ref, lng_ref, lnb_ref, ws_ref, bst_ref, wout_ref = refs[:8]
    cast_src = refs[8:8 + n_cast]
    o_ref = refs[8 + n_cast]
    cast_dst = refs[9 + n_cast:9 + 2 * n_cast]
    vn_ref, gated_ref = refs[9 + 2 * n_cast:]
    tm, width = vn_ref.shape
    head = width // A_GROUPS

    v = v_ref[...].astype(F32)
    mu = jnp.mean(v, axis=-1, keepdims=True)
    vc = v - mu
    var = jnp.mean(vc * vc, axis=-1, keepdims=True)
    vn_ref[...] = (vc * lax.rsqrt(var + EPS) * lng_ref[...] + lnb_ref[...]).astype(BF16)

    row = lax.broadcasted_iota(jnp.int32, (CHUNK, CHUNK), 0)
    col = lax.broadcasted_iota(jnp.int32, (CHUNK, CHUNK), 1)
    causal = (row >= col).astype(F32)
    for g in range(A_GROUPS):
        cols = slice(g * head, (g + 1) * head)
        w_g = (ws_ref[g] * causal).astype(BF16)
        b_g = bst_ref[:, g:g + 1]
        for c in range(tm // CHUNK):
            rows = slice(c * CHUNK, (c + 1) * CHUNK)
            s = jnp.dot(w_g, vn_ref[rows, cols], preferred_element_type=F32) + b_g
            gated_ref[rows, cols] = (u_ref[rows, cols].astype(F32) * s).astype(BF16)

    o_ref[...] = x_ref[...] + jnp.dot(gated_ref[...], wout_ref[...].astype(BF16),
                                      preferred_element_type=F32)
    _cast_ahead(cast_src, cast_dst)


def _gmlp_out(x, z, ln_g, ln_b, w_s, b_s_t, w_out, cast, *, tm):
    m, d = x.shape
    width = w_out.shape[0]
    c_in, c_out, c_shapes = _cast_ahead_specs(cast, m // tm, lambda i: i)
    return pl.pallas_call(
        functools.partial(_gmlp_out_kernel, n_cast=len(cast)),
        grid=(m // tm,),
        in_specs=[
            pl.BlockSpec((tm, d), lambda i: (i, 0)),
            pl.BlockSpec((tm, width), lambda i: (i, 0)),
            pl.BlockSpec((tm, width), lambda i: (i, 1)),
            _const_spec((1, width)),
            _const_spec((1, width)),
            _const_spec(w_s.shape),
            _const_spec(b_s_t.shape),
            _const_spec(w_out.shape),
        ] + c_in,
        out_specs=[pl.BlockSpec((tm, d), lambda i: (i, 0))] + c_out,
        out_shape=[jax.ShapeDtypeStruct((m, d), F32)] + c_shapes,
        scratch_shapes=[pltpu.VMEM((tm, width), BF16), pltpu.VMEM((tm, width), BF16)],
        compiler_params=pltpu.CompilerParams(
            dimension_semantics=("parallel",),
            vmem_limit_bytes=VMEM_LIMIT_BYTES),
        name="gmlp_out",
    )(x, z, z, ln_g, ln_b, w_s, b_s_t, w_out, *[cw for cw, _ in cast])


def _mlp_kernel(h_hbm, g_ref, w1_hbm, w2_hbm, fg_ref, o_ref, h_buf, hn_ref, w1_buf, w2_buf, sem, h_sem,
                *, layers, tf, apply_final_norm):
    i = pl.program_id(0)
    tm = h_buf.shape[0]
    w1_src = w1_hbm if layers[0] is None else w1_hbm.at[layers[0]]
    w2_src = w2_hbm if layers[1] is None else w2_hbm.at[layers[1]]

    def h_copy(tile_index):
        first = tile_index * tm
        if not isinstance(tile_index, int):
            first = pl.multiple_of(first, tm)
        return pltpu.make_async_copy(h_hbm.at[pl.ds(first, tm), :], h_buf, h_sem.at[0])

    n_tiles = w1_src.shape[1] // tf
    assert n_tiles % 2 == 0

    def copies(t, slot):
        off = t * tf if isinstance(t, int) else pl.multiple_of(t * tf, tf)
        return (
            pltpu.make_async_copy(w1_src.at[:, pl.ds(off, tf)], w1_buf.at[slot],
                                  sem.at[0, slot]),
            pltpu.make_async_copy(w2_src.at[pl.ds(off, tf), :], w2_buf.at[slot],
                                  sem.at[1, slot]),
        )

    def tile(buf, slot):
        w = buf[slot]
        return w if w.dtype == BF16 else w.astype(BF16)

    def start(t, slot):
        for c in copies(t, slot):
            c.start()

    def wait(t, slot):
        for c in copies(t, slot):
            c.wait()

    @pl.when(i == 0)
    def _():
        h_copy(0).start()
        start(0, 0)

    h_copy(i).wait()
    h = h_buf[...]
    hn_ref[...] = _rmsnorm(h, g_ref[...]).astype(BF16)
    o_ref[...] = h

    @pl.when(i + 1 < pl.num_programs(0))
    def _():
        h_copy(i + 1).start()

    def accumulate(slot):
        a = jnp.dot(hn_ref[...], tile(w1_buf, slot), preferred_element_type=F32)
        a = jnp.maximum(a, 0.0)
        o_ref[...] += jnp.dot((a * a).astype(BF16), tile(w2_buf, slot),
                              preferred_element_type=F32)

    def pair(p, carry):
        t = 2 * p
        start(t + 1, 1)
        wait(t, 0)
        accumulate(0)

        is_last_tile = t + 2 == n_tiles

        @pl.when(jnp.logical_not(jnp.logical_and(is_last_tile, i == pl.num_programs(0) - 1)))
        def _():
            start(jnp.where(is_last_tile, 0, t + 2), 0)

        wait(t + 1, 1)
        accumulate(1)
        return carry

    lax.fori_loop(0, n_tiles // 2, pair, 0)

    if apply_final_norm:
        o_ref[...] = _rmsnorm(o_ref[...], fg_ref[...])


def _mlp(h, g, w1, w2, final_g, layers, *, tm, tf, apply_final_norm):
    m, d = h.shape
    for w, layer in zip((w1, w2), layers):
        assert (layer is None) == (w.ndim == 2 and w.dtype == BF16)
    return pl.pallas_call(
        functools.partial(_mlp_kernel, layers=layers, tf=tf, apply_final_norm=apply_final_norm),
        grid=(m // tm,),
        in_specs=[
            pl.BlockSpec(memory_space=pl.ANY),
            pl.BlockSpec((1, d), lambda i: (0, 0)),
            pl.BlockSpec(memory_space=pl.ANY),
            pl.BlockSpec(memory_space=pl.ANY),
            pl.BlockSpec((1, d), lambda i: (0, 0)),
        ],
        out_specs=pl.BlockSpec((tm, d), lambda i: (i, 0)),
        out_shape=jax.ShapeDtypeStruct((m, d), F32),
        scratch_shapes=[
            pltpu.VMEM((tm, d), F32),
            pltpu.VMEM((tm, d), BF16),
            pltpu.VMEM((2, d, tf), w1.dtype),
            pltpu.VMEM((2, tf, d), w2.dtype),
            pltpu.SemaphoreType.DMA((2, 2)),
            pltpu.SemaphoreType.DMA((1,)),
        ],
        compiler_params=pltpu.CompilerParams(
            dimension_semantics=("arbitrary",),
            vmem_limit_bytes=VMEM_LIMIT_BYTES),
        name="mlp_final" if apply_final_norm else "mlp",
    )(h, g, w1, w2, final_g)


def _pool_kernel(*refs, seq, n_cast):
    x_ref, g_ref, win_ref, wgrp_ref, scale_ref, wout_ref = refs[:6]
    cast_src = refs[6:6 + n_cast]
    o_ref = refs[6 + n_cast]
    cast_dst = refs[7 + n_cast:7 + 2 * n_cast]
    vbuf_ref, mixed_ref = refs[7 + 2 * n_cast:]
    tm, width = mixed_ref.shape
    head = width // len(B_WINDOWS)
    i = pl.program_id(0)
    start = (i * tm) % seq

    @pl.when(start == 0)
    def _():
        vbuf_ref[0:HALO, :] = jnp.zeros((HALO, width), F32)

    @pl.when(start != 0)
    def _():
        vbuf_ref[0:HALO, :] = vbuf_ref[tm:tm + HALO, :]

    x = x_ref[...]
    hn = _rmsnorm(x, g_ref[...]).astype(BF16)
    vbuf_ref[HALO:HALO + tm, :] = jnp.dot(hn, win_ref[...].astype(BF16),
                                          preferred_element_type=F32)

    pos = start + lax.broadcasted_iota(jnp.int32, (tm, 1), 0)
    for g, w in enumerate(B_WINDOWS):
        cols = slice(g * head, (g + 1) * head)
        v = vbuf_ref[HALO:HALO + tm, cols]
        part = vbuf_ref[:, cols]
        span = 1
        while span < min(w, SUBLANES):
            part = part + pltpu.roll(part, span, axis=0)
            span *= 2
        total = part[HALO:, :]
        if w > SUBLANES:
            assert w == 2 * SUBLANES
            total = total + part[HALO - SUBLANES:HALO - SUBLANES + tm, :]
        inv_count = 1.0 / jnp.minimum(pos + 1, w).astype(F32)
        pooled = (total * inv_count - v).astype(BF16)
        mixed = jnp.dot(pooled, wgrp_ref[g].astype(BF16), preferred_element_type=F32)
        mixed_ref[:, cols] = (mixed * scale_ref[:, cols]).astype(BF16)

    o_ref[...] = x + jnp.dot(mixed_ref[...], wout_ref[...].astype(BF16),
                             preferred_element_type=F32)
    _cast_ahead(cast_src, cast_dst)


def _pool_mixer(x, g, w_in, w_grp, scale, w_out, cast, *, tm, seq):
    m, d = x.shape
    width = w_in.shape[1]
    c_in, c_out, c_shapes = _cast_ahead_specs(cast, m // tm, lambda i: i)
    return pl.pallas_call(
        functools.partial(_pool_kernel, seq=seq, n_cast=len(cast)),
        grid=(m // tm,),
        in_specs=[
            pl.BlockSpec((tm, d), lambda i: (i, 0)),
            _const_spec((1, d)),
            _const_spec(w_in.shape),
            _const_spec(w_grp.shape),
            _const_spec((1, width)),
            _const_spec(w_out.shape),
        ] + c_in,
        out_specs=[pl.BlockSpec((tm, d), lambda i: (i, 0))] + c_out,
        out_shape=[jax.ShapeDtypeStruct((m, d), F32)] + c_shapes,
        scratch_shapes=[pltpu.VMEM((HALO + tm, width), F32), pltpu.VMEM((tm, width), BF16)],
        compiler_params=pltpu.CompilerParams(
            dimension_semantics=("arbitrary",),
            vmem_limit_bytes=VMEM_LIMIT_BYTES),
        name="pool_mixer",
    )(x, g, w_in, w_grp, scale, w_out, *[cw for cw, _ in cast])


def kernel(x, a_w_in, a_ln_g, a_ln_b, a_w_s, a_b_s, a_w_out, b_w_in, b_w_grp, b_scale, b_w_out,
           norm_mix, norm_mlp, mlp_w1, mlp_w2, final_norm):
    bsz, seq, d = x.shape
    depth = norm_mix.shape[0]
    assert depth == 2
    h = x.reshape(bsz * seq, d)
    row = lambda p: p.reshape(1, -1)

    z, = _gmlp_in(h, row(norm_mix[0]), a_w_in, 0, [], tm=2048, tn=1024)
    h, w2_0 = _gmlp_out(h, z, row(a_ln_g[0]), row(a_ln_b[0]), a_w_s[0], a_b_s[0].T, a_w_out[0],
                        [(mlp_w2, 0)], tm=512)
    h = _mlp(h, row(norm_mlp[0]), mlp_w1, w2_0, row(final_norm), (0, None), tm=1024, tf=1024,
             apply_final_norm=False)

    h, w1_1, w2_1 = _pool_mixer(h, row(norm_mix[1]), b_w_in[0], b_w_grp[0], row(b_scale[0]),
                                b_w_out[0], [(mlp_w1, 1), (mlp_w2, 1)], tm=256, seq=seq)
    h = _mlp(h, row(norm_mlp[1]), w1_1, w2_1, row(final_norm), (None, None), tm=1024, tf=1024,
             apply_final_norm=True)
    return h.reshape(bsz, seq, d)
```
